```python
import functools
import jax, jax.numpy as jnp
from jax import lax
import numpy as np

D_MODEL = 1024
BATCH = 8
SEQ = 2048
DEPTH = 4
DEC_BATCH = 128
DEC_SEQ = 1
PAST_LEN = 2048
PAGE_SIZE = 128

HEAD_DIM = 64
H_SB = 8
H_MB = 8
SB_WIDTH = H_SB * HEAD_DIM
MB_WIDTH = H_MB * HEAD_DIM
MOBA_BLOCK = 256
MOBA_TOPK = 3
MOBA_QCHUNK = 64
SB_QBLOCK = 128
D_FF = ((-(-8 * D_MODEL // 3) + 255) // 256) * 256
N_IN = 3 * SB_WIDTH + 3 * MB_WIDTH + 2 * D_MODEL
RMS_EPS = 1e-6

kernel_name = "stickbreak_moba_gated_hybrid_step"


def rmsnorm(x, g):
    xf = x.astype(jnp.float32)
    y = xf * lax.rsqrt(jnp.mean(xf * xf, axis=-1, keepdims=True) + RMS_EPS)
    return (y * g.astype(jnp.float32)).astype(x.dtype)


def modulate(h, shift, scale):
    return h * (1.0 + scale[:, None, :]) + shift[:, None, :]


def alibi_slopes(n_heads):
    return 2.0 ** (-8.0 * jnp.arange(1, n_heads + 1, dtype=jnp.float32) / n_heads)


def project(h, w_in_l):
    z = h @ w_in_l
    sizes = (SB_WIDTH,) * 3 + (MB_WIDTH,) * 3 + (D_MODEL, D_MODEL)
    qs, ks, vs, qm, km, vm, gs, gm = jnp.split(z, [int(i) for i in np.cumsum(sizes)[:-1]], axis=-1)
    B, T = h.shape[:2]
    heads = lambda t, n: t.reshape(B, T, n, HEAD_DIM)
    return (heads(qs, H_SB), heads(ks, H_SB), heads(vs, H_SB),
            heads(qm, H_MB), heads(km, H_MB), heads(vm, H_MB), gs, gm)


def stick_breaking(q, k, v, q_pos, k_pos):
    z = jnp.einsum('bqhd,bkhd->bhqk', q, k, preferred_element_type=jnp.float32) * (HEAD_DIM ** -0.5)
    mask = k_pos[None, :] < q_pos[:, None]
    log_fail = jnp.where(mask, jax.nn.log_sigmoid(-z), 0.0)
    after = lax.cumsum(log_fail, axis=3, reverse=True) - log_fail
    a = jnp.where(mask, jnp.exp(jax.nn.log_sigmoid(z) + after), 0.0)
    return jnp.einsum('bhqk,bkhd->bqhd', a.astype(v.dtype), v)


def sb_prompt(q, k, v):
    B, T, H, D = q.shape
    n = T // SB_QBLOCK
    pos = jnp.arange(T)
    qb = q.reshape(B, n, SB_QBLOCK, H, D).transpose(1, 0, 2, 3, 4)
    pb = pos.reshape(n, SB_QBLOCK)
    out = lax.map(lambda a: stick_breaking(a[0], k, v, a[1], pos), (qb, pb))
    return out.transpose(1, 0, 2, 3, 4).reshape(B, T, H, D)


def sb_cached(q, k, v, past):
    P, T = past.shape[1], q.shape[1]
    k_all = jnp.concatenate([past[:, :, 0], k], axis=1)
    v_all = jnp.concatenate([past[:, :, 1], v], axis=1)
    return stick_breaking(q, k_all, v_all, P + jnp.arange(T), jnp.arange(P + T))


def moba_blocks(k, v):
    B, L, H, D = k.shape
    pad = (-L) % MOBA_BLOCK
    nb = (L + pad) // MOBA_BLOCK
    kb = jnp.pad(k, ((0, 0), (0, pad), (0, 0), (0, 0))).reshape(B, nb, MOBA_BLOCK, H, D)
    vb = jnp.pad(v, ((0, 0), (0, pad), (0, 0), (0, 0))).reshape(B, nb, MOBA_BLOCK, H, D)
    k_means = jnp.mean(kb.astype(jnp.float32), axis=2).astype(k.dtype)
    return kb, vb, k_means


def moba_attend(q, k_blocks, v_blocks, k_means, q_pos):
    B, Q, H, _ = q.shape
    nb = k_blocks.shape[1]
    scale = HEAD_DIM ** -0.5
    slopes = alibi_slopes(H)[None, None, :, None]
    own = q_pos // MOBA_BLOCK
    gate = jnp.einsum('bqhd,bnhd->bqhn', q, k_means, preferred_element_type=jnp.float32)
    fully_past = jnp.arange(nb)[None, :] < own[:, None]
    gate = jnp.where(fully_past[None, :, None, :], gate, -jnp.inf)
    _, top = lax.top_k(gate, min(MOBA_TOPK, nb))
    chosen_ok = top < own[None, :, None, None]
    bi = jnp.arange(B)[:, None, None]
    hi = jnp.arange(H)[None, None, :]
    t = q_pos[None, :, None, None]
    offs = jnp.arange(MOBA_BLOCK)

    def block_scores(blk, ok):
        kb = k_blocks[bi, blk, :, hi]
        vb = v_blocks[bi, blk, :, hi]
        kpos = blk[..., None] * MOBA_BLOCK + offs
        s = (jnp.einsum('bqhd,bqhkd->bqhk', q, kb, preferred_element_type=jnp.float32) * scale
             - slopes * (t - kpos).astype(jnp.float32))
        keep = (kpos <= t) & ok[..., None]
        return jnp.where(keep, s, -jnp.inf), vb

    own_blk = jnp.broadcast_to(own[None, :, None], (B, Q, H))
    s, vb = block_scores(own_blk, jnp.ones((), dtype=bool))
    m = jnp.max(s, axis=-1)
    p = jnp.exp(s - m[..., None])
    l = jnp.sum(p, axis=-1)
    acc = jnp.einsum('bqhk,bqhkd->bqhd', p, vb.astype(jnp.float32))
    for i in range(top.shape[-1]):
        s, vb = block_scores(top[..., i], chosen_ok[..., i])
        m_new = jnp.maximum(m, jnp.max(s, axis=-1))
        corr = jnp.exp(m - m_new)
        p = jnp.exp(s - m_new[..., None])
        l = l * corr + jnp.sum(p, axis=-1)
        acc = acc * corr[..., None] + jnp.einsum('bqhk,bqhkd->bqhd', p, vb.astype(jnp.float32))
        m = m_new
    return (acc / l[..., None]).astype(q.dtype)


def moba_prompt(q, k, v):
    B, T, H, D = q.shape
    kb, vb, km = moba_blocks(k, v)
    n = T // MOBA_QCHUNK
    qc = q.reshape(B, n, MOBA_QCHUNK, H, D).transpose(1, 0, 2, 3, 4)
    pc = jnp.arange(T).reshape(n, MOBA_QCHUNK)
    out = lax.map(lambda a: moba_attend(a[0], kb, vb, km, a[1]), (qc, pc))
    return out.transpose(1, 0, 2, 3, 4).reshape(B, T, H, D)


def moba_cached(q, k, v, past):
    P, T = past.shape[1], q.shape[1]
    k_all = jnp.concatenate([past[:, :, 0], k], axis=1)
    v_all = jnp.concatenate([past[:, :, 1], v], axis=1)
    kb, vb, km = moba_blocks(k_all, v_all)
    return moba_attend(q, kb, vb, km, P + jnp.arange(T))


def swiglu(h, w_gu, w_dn):
    g, u = jnp.split(h @ w_gu, 2, axis=-1)
    return (jax.nn.silu(g) * u) @ w_dn


def layer(x, c, sb_fn, mb_fn, w_ada_l, b_ada_l, g_mix_l, w_in_l, w_ps_l, w_pm_l, w_o_l, g_ffn_l, w_gu_l, w_dn_l):
    sh1, sc1, ga1, sh2, sc2, ga2 = jnp.split(jax.nn.silu(c) @ w_ada_l + b_ada_l, 6, axis=-1)
    B, T = x.shape[:2]
    h = modulate(rmsnorm(x, g_mix_l), sh1, sc1)
    qs, ks, vs, qm, km, vm, gs, gm = project(h, w_in_l)
    o_sb = sb_fn(qs, ks, vs)
    o_mb = mb_fn(qm, km, vm)
    merged = (jax.nn.sigmoid(gs) * (o_sb.reshape(B, T, SB_WIDTH) @ w_ps_l)
              + jax.nn.sigmoid(gm) * (o_mb.reshape(B, T, MB_WIDTH) @ w_pm_l))
    x = x + ga1[:, None, :] * (merged @ w_o_l)
    h = modulate(rmsnorm(x, g_ffn_l), sh2, sc2)
    x = x + ga2[:, None, :] * swiglu(h, w_gu_l, w_dn_l)
    return x, jnp.stack([ks, vs], axis=2), jnp.stack([km, vm], axis=2)


def setup_inputs(seed: int = 0) -> dict:
    key = jax.random.key(seed)
    ks = jax.random.split(key, 20)
    n_pages = PAST_LEN // PAGE_SIZE
    n_pool = (DEC_BATCH * n_pages * 5) // 4
    nrm = lambda k, shape, s=1.0: s * jax.random.normal(k, shape, jnp.float32)
    page_table = jax.random.permutation(ks[6], n_pool)[: DEC_BATCH * n_pages].reshape(DEC_BATCH, n_pages).astype(jnp.int32)
    D = D_MODEL
    return {
        "x_prompt": nrm(ks[0], (BATCH, SEQ, D)),
        "x_sample": nrm(ks[1], (DEC_BATCH, DEC_SEQ, D)),
        "c_prompt": nrm(ks[2], (BATCH, D)),
        "c_sample": nrm(ks[3], (DEC_BATCH, D)),
        "cache_sb_kv": nrm(ks[4], (DEPTH, n_pool, PAGE_SIZE, 2, H_SB, HEAD_DIM)),
        "cache_moba_kv": nrm(ks[5], (DEPTH, n_pool, PAGE_SIZE, 2, H_MB, HEAD_DIM)),
        "page_table": page_table,
        "w_ada": nrm(ks[7], (DEPTH, D, 6 * D), 0.5 * D ** -0.5),
        "b_ada": nrm(ks[8], (DEPTH, 6 * D), 0.02),
        "g_mix": 1.0 + nrm(ks[9], (DEPTH, D), 0.05),
        "w_in": nrm(ks[10], (DEPTH, D, N_IN), D ** -0.5),
        "w_proj_sb": nrm(ks[11], (DEPTH, SB_WIDTH, D), SB_WIDTH ** -0.5),
        "w_proj_moba": nrm(ks[12], (DEPTH, MB_WIDTH, D), MB_WIDTH ** -0.5),
        "w_out": nrm(ks[13], (DEPTH, D, D), D ** -0.5),
        "g_ffn": 1.0 + nrm(ks[14], (DEPTH, D), 0.05),
        "w_gate_up": nrm(ks[15], (DEPTH, D, 2 * D_FF), D ** -0.5),
        "w_down": nrm(ks[16], (DEPTH, D_FF, D), D_FF ** -0.5),
        "g_final": 1.0 + nrm(ks[17], (D,), 0.05),
    }


def reference(x_prompt, x_sample, c_prompt, c_sample, cache_sb_kv, cache_moba_kv, page_table,
              w_ada, b_ada, g_mix, w_in, w_proj_sb, w_proj_moba, w_out, g_ffn, w_gate_up, w_down, g_final):
    n_seq = page_table.shape[0]
    past_len = page_table.shape[1] * cache_sb_kv.shape[2]
    xp, xs = x_prompt, x_sample
    sb_p, mb_p, sb_s, mb_s = [], [], [], []
    for l in range(DEPTH):
        w = (w_ada[l], b_ada[l], g_mix[l], w_in[l], w_proj_sb[l], w_proj_moba[l], w_out[l],
             g_ffn[l], w_gate_up[l], w_down[l])
        xp, kv_sb, kv_mb = layer(xp, c_prompt, sb_prompt, moba_prompt, *w)
        sb_p.append(kv_sb)
        mb_p.append(kv_mb)
        past_sb = cache_sb_kv[l, page_table].reshape(n_seq, past_len, 2, H_SB, HEAD_DIM)
        past_mb = cache_moba_kv[l, page_table].reshape(n_seq, past_len, 2, H_MB, HEAD_DIM)
        xs, kv_sb, kv_mb = layer(xs, c_sample,
                                 functools.partial(sb_cached, past=past_sb),
                                 functools.partial(moba_cached, past=past_mb), *w)
        sb_s.append(kv_sb)
        mb_s.append(kv_mb)
    y_prompt = rmsnorm(xp, g_final)
    y_sample = rmsnorm(xs, g_final)
    return (y_prompt, y_sample, jnp.stack(sb_p, 0), jnp.stack(mb_p, 0), jnp.stack(sb_s, 0), jnp.stack(mb_s, 0))
```

```python
import functools

import jax
import jax.numpy as jnp
from jax import lax
from jax.experimental import pallas as pl
from jax.experimental.pallas import tpu as pltpu

HEAD_DIM = 64
N_HEADS = 8
WIDTH = N_HEADS * HEAD_DIM
MOBA_BLOCK = 256
MOBA_TOPK = 3
RMS_EPS = 1e-6
ATT_SCALE = HEAD_DIM ** -0.5
ALIBI_SLOPES = tuple(2.0 ** (-8.0 * (h + 1) / N_HEADS) for h in range(N_HEADS))
NEG = -1e30
MXU_DTYPE = jnp.bfloat16
F32 = jnp.float32
VMEM_LIMIT_BYTES = 56 * 1024 * 1024
NT_DIMS = (((1,), (1,)), ((), ()))


def _params(*semantics):
    return pltpu.CompilerParams(dimension_semantics=semantics, vmem_limit_bytes=VMEM_LIMIT_BYTES)


def _mm(a, b):
    return jnp.dot(a, b, preferred_element_type=F32)


def _mm_nt(a, b):
    return lax.dot_general(a, b, NT_DIMS, preferred_element_type=F32)


def _neg_softplus(z):
    return -(jnp.maximum(z, 0.0) + jnp.log1p(jnp.exp(-jnp.abs(z))))


def _split_hi_lo(x):
    hi = x.astype(MXU_DTYPE)
    lo = (x - hi.astype(F32)).astype(MXU_DTYPE)
    return hi, lo


def _rms_modulate(x, g, shift, scale):
    y = x * lax.rsqrt(jnp.mean(x * x, axis=-1, keepdims=True) + RMS_EPS)
    return (y * g) * (1.0 + scale) + shift


def _ada_kernel(c_ref, w_ref, b_ref, o_ref):
    c = c_ref[...]
    a = (c * jax.nn.sigmoid(c)).astype(MXU_DTYPE)
    o_ref[...] = _mm(a, w_ref[...].astype(MXU_DTYPE)) + b_ref[...]


def _ada_call(c_all, w_ada, b_ada, tn=1536):
    depth, d, n = w_ada.shape
    m = c_all.shape[0]
    return pl.pallas_call(
        _ada_kernel,
        out_shape=jax.ShapeDtypeStruct((depth, m, n), F32),
        grid=(depth, n // tn),
        in_specs=[
            pl.BlockSpec((m, d), lambda l, j: (0, 0)),
            pl.BlockSpec((None, d, tn), lambda l, j: (l, 0, j)),
            pl.BlockSpec((None, 1, tn), lambda l, j: (l, 0, j)),
        ],
        out_specs=pl.BlockSpec((None, m, tn), lambda l, j: (l, 0, j)),
        compiler_params=_params("arbitrary", "arbitrary"),
        name="adaln",
    )(c_all, w_ada, b_ada.reshape(depth, 1, n))


def _inproj_kernel(x_ref, g_ref, sh_ref, sc_ref, wq_ref, wkvt_ref, wg_ref, qs_ref, qm_ref, kvts_ref, kvtm_ref,
                   gate_ref):
    hb = _rms_modulate(x_ref[...], g_ref[...], sh_ref[...], sc_ref[...]).astype(MXU_DTYPE)
    w = WIDTH
    qs_ref[...] = (_mm(hb, wq_ref[:, :w]) * ATT_SCALE).astype(qs_ref.dtype)
    qm_ref[...] = (_mm(hb, wq_ref[:, w:]) * ATT_SCALE).astype(qm_ref.dtype)
    kvts_ref[...] = _mm_nt(wkvt_ref[:2 * w, :], hb)
    kvtm_ref[...] = _mm_nt(wkvt_ref[2 * w:, :], hb)
    gate_ref[...] = _mm(hb, wg_ref[...])


def _row_spec(tm, d):
    return pl.BlockSpec((None, tm, d), lambda b, i: (b, i, 0))


def _mod_spec(mod, tm, d):
    if mod.shape[1] == 1:
        return pl.BlockSpec((None, 1, d), lambda b, i: (b, 0, 0))
    return pl.BlockSpec((None, tm, d), lambda b, i: (b, i, 0))


def _full_spec(a):
    nd = a.ndim
    return pl.BlockSpec(a.shape, lambda b, i: (0,) * nd)


def _inproj_call(x, g, shift, scale, w_q, w_kvt, w_g, tm):
    b, t, d = x.shape
    row = lambda width, dtype: jax.ShapeDtypeStruct((b, t, width), dtype)
    kvt = jax.ShapeDtypeStruct((b, 2 * WIDTH, t), F32)
    kvt_spec = pl.BlockSpec((None, 2 * WIDTH, tm), lambda bb, i: (bb, 0, i))
    return pl.pallas_call(
        _inproj_kernel,
        out_shape=(row(WIDTH, MXU_DTYPE), row(WIDTH, MXU_DTYPE), kvt, kvt, row(w_g.shape[-1], F32)),
        grid=(b, t // tm),
        in_specs=[_row_spec(tm, d), _full_spec(g), _mod_spec(shift, tm, d), _mod_spec(scale, tm, d),
                  _full_spec(w_q), _full_spec(w_kvt), _full_spec(w_g)],
        out_specs=(_row_spec(tm, WIDTH), _row_spec(tm, WIDTH), kvt_spec, kvt_spec, _row_spec(tm, w_g.shape[-1])),
        compiler_params=_params("arbitrary", "arbitrary"),
        name="inproj",
    )(x, g, shift, scale, w_q, w_kvt, w_g)


def _stage_kv(kvt_ref, kb_ref, vt_ref, tile, km_ref=None):
    for c in range(kb_ref.shape[0]):
        k_rows = kvt_ref[:WIDTH, c * tile:(c + 1) * tile].T
        kb_ref[c] = k_rows.astype(MXU_DTYPE)
        vt_ref[c] = kvt_ref[WIDTH:, c * tile:(c + 1) * tile].astype(MXU_DTYPE)
        if km_ref is not None:
            km_ref[c:c + 1, :] = (jnp.sum(k_rows, axis=0, keepdims=True) * (1.0 / tile)).astype(MXU_DTYPE)


def _sb_prompt_kernel(q_ref, kv_ref, o_ref, kb_ref, vt_ref, ot_ref, *, tile):
    i = pl.program_id(1)

    @pl.when(i == 0)
    def _():
        _stage_kv(kv_ref, kb_ref, vt_ref, tile)

    row = lax.broadcasted_iota(jnp.int32, (tile, tile), 0)
    col = lax.broadcasted_iota(jnp.int32, (tile, tile), 1)
    before = row < col
    later = jnp.where(before, 1.0, 0.0).astype(MXU_DTYPE)

    def tile_terms(qh, kh, vth, carry, mask):
        z = _mm_nt(kh, qh)
        lf = _neg_softplus(z)
        lfm = lf if mask is None else jnp.where(mask, lf, 0.0)
        hi, lo = _split_hi_lo(lfm)
        after = carry + _mm(later, hi) + _mm(later, lo)
        a = jnp.exp(z + lf + after)
        if mask is not None:
            a = jnp.where(mask, a, 0.0)
        return _mm(vth, a.astype(MXU_DTYPE)), carry + jnp.sum(lfm, axis=0, keepdims=True)

    for h in range(N_HEADS):
        hs = slice(h * HEAD_DIM, (h + 1) * HEAD_DIM)
        qh = q_ref[:, hs]
        acc, carry = tile_terms(qh, kb_ref[i, :, hs], vt_ref[i, hs, :], jnp.zeros((1, tile), F32), before)

        def body(s, state, qh=qh, hs=hs):
            acc, carry = state
            j = i - 1 - s
            contrib, carry = tile_terms(qh, kb_ref[j, :, hs], vt_ref[j, hs, :], carry, None)
            return acc + contrib, carry

        acc, carry = lax.fori_loop(0, i, body, (acc, carry))
        ot_ref[hs, :] = acc
    o_ref[...] = ot_ref[...].T.astype(o_ref.dtype)


def _moba_prompt_kernel(q_ref, kv_ref, o_ref, kb_ref, vt_ref, km_ref, nb_ref, ot_ref, *, tile):
    i = pl.program_id(1)
    n_blk = kb_ref.shape[0]

    @pl.when(i == 0)
    def _():
        _stage_kv(kv_ref, kb_ref, vt_ref, tile, km_ref)

    row = lax.broadcasted_iota(jnp.int32, (tile, tile), 0)
    col = lax.broadcasted_iota(jnp.int32, (tile, tile), 1)
    rel = (row - col).astype(F32)
    causal = row <= col
    blk = lax.broadcasted_iota(jnp.int32, (n_blk, tile), 0)

    for h in range(N_HEADS):
        hs = slice(h * HEAD_DIM, (h + 1) * HEAD_DIM)
        slope = ALIBI_SLOPES[h]
        qh = q_ref[:, hs]

        gate = _mm_nt(km_ref[:, hs], qh)
        rank = jnp.zeros((n_blk, tile), F32)
        for n2 in range(n_blk):
            g2 = gate[n2:n2 + 1, :]
            beats = (g2 > gate) | ((g2 == gate) & (n2 < blk))
            rank = rank + jnp.where(beats, 1.0, 0.0) * jnp.where(n2 < i, 1.0, 0.0)
        chosen = (blk < i) & (rank < MOBA_TOPK)
        nb_ref[...] = jnp.where(chosen, 0.0, NEG)

        alibi = slope * rel
        s = _mm_nt(kb_ref[i, :, hs], qh) + jnp.where(causal, alibi, NEG)
        m = jnp.max(s, axis=0, keepdims=True)
        p = jnp.exp(s - m)
        l = jnp.sum(p, axis=0, keepdims=True)
        acc = _mm(vt_ref[i, hs, :], p.astype(MXU_DTYPE))

        def body(j, state, qh=qh, hs=hs, alibi=alibi, slope=slope):
            m, l, acc = state
            shift = (slope * tile) * (j - i).astype(F32)
            s = _mm_nt(kb_ref[j, :, hs], qh) + alibi + (nb_ref[pl.ds(j, 1), :] + shift)
            m_new = jnp.maximum(m, jnp.max(s, axis=0, keepdims=True))
            corr = jnp.exp(m - m_new)
            p = jnp.exp(s - m_new)
            l = l * corr + jnp.sum(p, axis=0, keepdims=True)
            acc = acc * corr + _mm(vt_ref[j, hs, :], p.astype(MXU_DTYPE))
            return m_new, l, acc

        m, l, acc = lax.fori_loop(0, i, body, (m, l, acc))
        ot_ref[hs, :] = acc / l
    o_ref[...] = ot_ref[...].T.astype(o_ref.dtype)


def _prompt_attention_call(kernel_fn, q, kv, extra_scratch, name):
    b, t, _ = q.shape
    tile = MOBA_BLOCK
    assert t % tile == 0
    n_t = t // tile
    scratch = [pltpu.VMEM((n_t, tile, WIDTH), MXU_DTYPE), pltpu.VMEM((n_t, WIDTH, tile), MXU_DTYPE)]
    scratch += extra_scratch(n_t, tile) + [pltpu.VMEM((WIDTH, tile), F32)]
    return pl.pallas_call(
        functools.partial(kernel_fn, tile=tile),
        out_shape=jax.ShapeDtypeStruct((b, t, WIDTH), MXU_DTYPE),
        grid=(b, n_t),
        in_specs=[pl.BlockSpec((None, tile, WIDTH), lambda bb, i: (bb, i, 0)),
                  pl.BlockSpec((None, 2 * WIDTH, t), lambda bb, i: (bb, 0, 0))],
        out_specs=pl.BlockSpec((None, tile, WIDTH), lambda bb, i: (bb, i, 0)),
        scratch_shapes=scratch,
        compiler_params=_params("arbitrary", "arbitrary"),
        name=name,
    )(q, kv)


def _sb_prompt_call(q, kv):
    return _prompt_attention_call(_sb_prompt_kernel, q, kv, lambda n_t, tile: [], "sb_prompt")


def _moba_prompt_call(q, kv):
    extra = lambda n_t, tile: [pltpu.VMEM((n_t, WIDTH), MXU_DTYPE), pltpu.VMEM((n_t, tile), F32)]
    return _prompt_attention_call(_moba_prompt_kernel, q, kv, extra, "moba_prompt")


def _decode_kernel(pt_ref, layer_ref, qs_ref, qm_ref, kvtn_ref, *refs, n_pages, page):
    del pt_ref, layer_ref
    b = pl.program_id(0)
    sb_pages, mb_pages = refs[:n_pages], refs[n_pages:2 * n_pages]
    osb_ref, omb_ref, km_ref, kvn_ref = refs[2 * n_pages:]
    past = n_pages * page
    pages_per_blk = MOBA_BLOCK // page
    n_blk = past // MOBA_BLOCK

    @pl.when(b == 0)
    def _():
        kvn_ref[...] = kvtn_ref[...].T

    head_of_lane = lax.broadcasted_iota(jnp.int32, (N_HEADS, WIDTH), 1) // HEAD_DIM
    head_row = lax.broadcasted_iota(jnp.int32, (N_HEADS, WIDTH), 0)
    own_cols = head_of_lane == head_row

    def spread(q_row):
        return jnp.where(own_cols, q_row.astype(F32), 0.0)

    def gather_heads(x):
        return jnp.sum(jnp.where(own_cols, x, 0.0), axis=0, keepdims=True)

    qb = spread(qs_ref[...]).astype(MXU_DTYPE)
    r = lax.broadcasted_iota(jnp.int32, (page, 2 * page), 0)
    c = lax.broadcasted_iota(jnp.int32, (page, 2 * page), 1)
    later_and_total = jnp.where((r > c) | (c >= page), 1.0, 0.0).astype(MXU_DTYPE)
    carry = jnp.zeros((N_HEADS, page), F32)
    acc = jnp.zeros((N_HEADS, WIDTH), F32)
    for p in reversed(range(n_pages)):
        z = _mm(qb, sb_pages[p][:WIDTH, :].astype(MXU_DTYPE))
        lf = _neg_softplus(z)
        hi, lo = _split_hi_lo(lf)
        sums = _mm(hi, later_and_total) + _mm(lo, later_and_total)
        a = jnp.exp(z + lf + carry + sums[:, :page])
        acc = acc + _mm_nt(a.astype(MXU_DTYPE), sb_pages[p][WIDTH:, :].astype(MXU_DTYPE))
        carry = carry + sums[:, page:]
    osb_ref[...] = gather_heads(acc).astype(osb_ref.dtype)

    qb = spread(qm_ref[...]).astype(MXU_DTYPE)
    for n in range(n_blk):
        tot = mb_pages[n * pages_per_blk][:WIDTH, :]
        for p in range(n * pages_per_blk + 1, (n + 1) * pages_per_blk):
            tot = tot + mb_pages[p][:WIDTH, :]
        km_ref[:, n:n + 1] = jnp.sum(tot, axis=1, keepdims=True) * (1.0 / MOBA_BLOCK)
    gate = _mm(qb, km_ref[...].astype(MXU_DTYPE))
    blk = lax.broadcasted_iota(jnp.int32, (N_HEADS, n_blk), 1)
    rank = jnp.zeros((N_HEADS, n_blk), F32)
    for n2 in range(n_blk):
        g2 = gate[:, n2:n2 + 1]
        beats = (g2 > gate) | ((g2 == gate) & (n2 < blk))
        rank = rank + jnp.where(beats, 1.0, 0.0)
    not_chosen = jnp.where(rank < MOBA_TOPK, 0.0, NEG)

    slope = jnp.zeros((N_HEADS, page), F32)
    hrow = lax.broadcasted_iota(jnp.int32, (N_HEADS, page), 0)
    for h in range(N_HEADS):
        slope = jnp.where(hrow == h, ALIBI_SLOPES[h], slope)
    lane = lax.broadcasted_iota(jnp.int32, (N_HEADS, page), 1)

    kv_new = kvn_ref[pl.ds(b, 1), :].astype(MXU_DTYPE).astype(F32)
    s_own = jnp.sum(qb.astype(F32) * kv_new[:, :WIDTH], axis=1, keepdims=True)
    scores = []
    m = s_own
    for p in range(n_pages):
        dist = (past - p * page - lane).astype(F32)
        n = p // pages_per_blk
        s = _mm(qb, mb_pages[p][:WIDTH, :].astype(MXU_DTYPE)) - slope * dist + not_chosen[:, n:n + 1]
        scores.append(s)
        m = jnp.maximum(m, jnp.max(s, axis=1, keepdims=True))
    p_own = jnp.exp(s_own - m)
    l = p_own
    acc = p_own.astype(MXU_DTYPE).astype(F32) * kv_new[:, WIDTH:]
    for p in range(n_pages):
        w = jnp.exp(scores[p] - m)
        l = l + jnp.sum(w, axis=1, keepdims=True)
        acc = acc + _mm_nt(w.astype(MXU_DTYPE), mb_pages[p][WIDTH:, :].astype(MXU_DTYPE))
    omb_ref[...] = gather_heads(acc / l).astype(omb_ref.dtype)


def _decode_call(layer, page_table, q_sb, q_mb, kvt_mb_new, cache_sb, cache_mb):
    n_seq, n_pages = page_table.shape
    page = cache_sb.shape[-1]
    assert (n_pages * page) % MOBA_BLOCK == 0 and MOBA_BLOCK % page == 0

    def page_spec(p):
        return pl.BlockSpec((None, None, 2 * WIDTH, page),
                            lambda b, pt, lay: (lay[0], pt[b * n_pages + p], 0, 0))

    def seq_spec(width):
        return pl.BlockSpec((None, 1, width), lambda b, pt, lay: (b, 0, 0))

    grid_spec = pltpu.PrefetchScalarGridSpec(
        num_scalar_prefetch=2,
        grid=(n_seq,),
        in_specs=[seq_spec(WIDTH), seq_spec(WIDTH), pl.BlockSpec((2 * WIDTH, n_seq), lambda b, pt, lay: (0, 0))]
        + [page_spec(p) for p in range(n_pages)] * 2,
        out_specs=(seq_spec(WIDTH), seq_spec(WIDTH)),
        scratch_shapes=[pltpu.VMEM((WIDTH, n_pages * page // MOBA_BLOCK), F32),
                        pltpu.VMEM((n_seq, 2 * WIDTH), F32)],
    )
    out_shape = (jax.ShapeDtypeStruct((n_seq, 1, WIDTH), MXU_DTYPE),) * 2
    return pl.pallas_call(
        functools.partial(_decode_kernel, n_pages=n_pages, page=page),
        out_shape=out_shape,
        grid_spec=grid_spec,
        compiler_params=_params("arbitrary"),
        name="decode_attention",
    )(page_table.reshape(-1), jnp.full((1,), layer, jnp.int32), q_sb, q_mb, kvt_mb_new,
      *([cache_sb] * n_pages), *([cache_mb] * n_pages))


def _merge_kernel(osb_ref, omb_ref, gate_ref, x_ref, ga_ref, wps_ref, wpm_ref, wo_ref, xo_ref):
    d = x_ref.shape[-1]
    merged = (jax.nn.sigmoid(gate_ref[:, :d]) * _mm(osb_ref[...], wps_ref[...])
              + jax.nn.sigmoid(gate_ref[:, d:]) * _mm(omb_ref[...], wpm_ref[...]))
    xo_ref[...] = x_ref[...] + ga_ref[...] * _mm(merged.astype(MXU_DTYPE), wo_ref[...])


def _merge_call(o_sb, o_mb, gates, x, ga1, w_ps, w_pm, w_o, tm):
    b, t, d = x.shape
    return pl.pallas_call(
        _merge_kernel,
        out_shape=jax.ShapeDtypeStruct(x.shape, F32),
        grid=(b, t // tm),
        in_specs=[_row_spec(tm, WIDTH), _row_spec(tm, WIDTH), _row_spec(tm, gates.shape[-1]), _row_spec(tm, d),
                  _mod_spec(ga1, tm, d), _full_spec(w_ps), _full_spec(w_pm), _full_spec(w_o)],
        out_specs=_row_spec(tm, d),
        compiler_params=_params("arbitrary", "arbitrary"),
        name="merge_outproj",
    )(o_sb, o_mb, gates, x, ga1, w_ps, w_pm, w_o)


def _ffn_kernel(x_ref, g_ref, sh_ref, sc_ref, ga_ref, wgu_ref, wdn_ref, xo_ref, acc_ref):
    x = x_ref[...]
    hb = _rms_modulate(x, g_ref[...], sh_ref[...], sc_ref[...]).astype(MXU_DTYPE)
    n_chunks = wdn_ref.shape[0]
    for c in range(n_chunks):
        gate = _mm(hb, wgu_ref[c])
        up = _mm(hb, wgu_ref[n_chunks + c])
        act = (gate * jax.nn.sigmoid(gate) * up).astype(MXU_DTYPE)
        part = _mm(act, wdn_ref[c])
        if c == 0:
            acc_ref[...] = part
        else:
            acc_ref[...] += part
    xo_ref[...] = x + ga_ref[...] * acc_ref[...]


def _ffn_call(x, g, shift, scale, ga2, w_gu, w_dn, tm):
    b, t, d = x.shape
    return pl.pallas_call(
        _ffn_kernel,
        out_shape=jax.ShapeDtypeStruct(x.shape, F32),
        grid=(b, t // tm),
        in_specs=[_row_spec(tm, d), _full_spec(g), _mod_spec(shift, tm, d), _mod_spec(scale, tm, d),
                  _mod_spec(ga2, tm, d), _full_spec(w_gu), _full_spec(w_dn)],
        out_specs=_row_spec(tm, d),
        scratch_shapes=[pltpu.VMEM((tm, d), F32)],
        compiler_params=_params("arbitrary", "arbitrary"),
        name="swiglu",
    )(x, g, shift, scale, ga2, w_gu, w_dn)


def _final_norm_kernel(x_ref, g_ref, o_ref):
    x = x_ref[...]
    o_ref[...] = x * lax.rsqrt(jnp.mean(x * x, axis=-1, keepdims=True) + RMS_EPS) * g_ref[...]


def _final_norm_call(x, g, tm):
    b, t, d = x.shape
    return pl.pallas_call(
        _final_norm_kernel,
        out_shape=jax.ShapeDtypeStruct(x.shape, F32),
        grid=(b, t // tm),
        in_specs=[_row_spec(tm, d), _full_spec(g)],
        out_specs=_row_spec(tm, d),
        compiler_params=_params("arbitrary", "arbitrary"),
        name="final_norm",
    )(x, g)


FFN_CHUNK = 256


def _row_tile(t):
    for tm in (512, 256, 128, 64, 32, 16, 8):
        if t % tm == 0:
            return tm
    return t


def kernel(x_prompt, x_sample, c_prompt, c_sample, cache_sb_kv, cache_moba_kv, page_table, w_ada, b_ada, g_mix,
           w_in, w_proj_sb, w_proj_moba, w_out, g_ffn, w_gate_up, w_down, g_final):
    n_b, t, d = x_prompt.shape
    n_s = x_sample.shape[0]
    depth, n_pool, page = cache_sb_kv.shape[:3]
    d_ff = w_down.shape[1]
    assert x_sample.shape[1] == 1 and d_ff % FFN_CHUNK == 0
    n_chunks = d_ff // FFN_CHUNK
    w = WIDTH

    w_in_b = w_in.astype(MXU_DTYPE)
    w_q = jnp.concatenate([w_in_b[:, :, 0:w], w_in_b[:, :, 3 * w:4 * w]], axis=2)
    w_kvt = jnp.concatenate([w_in_b[:, :, w:3 * w], w_in_b[:, :, 4 * w:6 * w]], axis=2).transpose(0, 2, 1)
    w_g = w_in_b[:, :, 6 * w:]
    w_ps_b = w_proj_sb.astype(MXU_DTYPE)
    w_pm_b = w_proj_moba.astype(MXU_DTYPE)
    w_o_b = w_out.astype(MXU_DTYPE)
    w_gu_b = (w_gate_up.astype(MXU_DTYPE).reshape(depth, d, 2 * n_chunks, FFN_CHUNK).transpose(0, 2, 1, 3))
    w_dn_b = w_down.astype(MXU_DTYPE).reshape(depth, n_chunks, FFN_CHUNK, d)

    to_pages = lambda cache: cache.transpose(0, 1, 3, 4, 5, 2).reshape(depth, n_pool, 2 * w, page)
    cache_sb_t, cache_mb_t = to_pages(cache_sb_kv), to_pages(cache_moba_kv)

    mod = _ada_call(jnp.concatenate([c_prompt, c_sample], axis=0), w_ada, b_ada)

    xp = x_prompt
    xs = x_sample.reshape(1, n_s, d)
    tm_p, tm_s = _row_tile(t), _row_tile(n_s)
    outs = {k: [] for k in ("sb_p", "mb_p", "sb_s", "mb_s")}
    for l in range(depth):
        mod_p = [mod[l, :n_b, k * d:(k + 1) * d].reshape(n_b, 1, d) for k in range(6)]
        mod_s = [mod[l, n_b:, k * d:(k + 1) * d].reshape(1, n_s, d) for k in range(6)]
        g_mix_l, g_ffn_l = g_mix[l].reshape(1, d), g_ffn[l].reshape(1, d)

        qs, qm, kvts, kvtm, gates = _inproj_call(xp, g_mix_l, mod_p[0], mod_p[1], w_q[l], w_kvt[l], w_g[l], tm_p)
        o_sb = _sb_prompt_call(qs, kvts)
        o_mb = _moba_prompt_call(qm, kvtm)
        xp = _merge_call(o_sb, o_mb, gates, xp, mod_p[2], w_ps_b[l], w_pm_b[l], w_o_b[l], tm_p)
        xp = _ffn_call(xp, g_ffn_l, mod_p[3], mod_p[4], mod_p[5], w_gu_b[l], w_dn_b[l], tm_p)
        outs["sb_p"].append(kvts)
        outs["mb_p"].append(kvtm)

        qs, qm, kvts, kvtm, gates = _inproj_call(xs, g_mix_l, mod_s[0], mod_s[1], w_q[l], w_kvt[l], w_g[l], tm_s)
        o_sb, o_mb = _decode_call(l, page_table, qs.reshape(n_s, 1, w), qm.reshape(n_s, 1, w), kvtm[0],
                                  cache_sb_t, cache_mb_t)
        xs = _merge_call(o_sb.reshape(1, n_s, w), o_mb.reshape(1, n_s, w), gates, xs, mod_s[2],
                         w_ps_b[l], w_pm_b[l], w_o_b[l], tm_s)
        xs = _ffn_call(xs, g_ffn_l, mod_s[3], mod_s[4], mod_s[5], w_gu_b[l], w_dn_b[l], tm_s)
        outs["sb_s"].append(kvts)
        outs["mb_s"].append(kvtm)

    g_fin = g_final.reshape(1, d)
    y_prompt = _final_norm_call(xp, g_fin, tm_p)
    y_sample = _final_norm_call(xs, g_fin, tm_s).reshape(n_s, 1, d)

    def kv_prompt(parts):
        return jnp.stack(parts, 0).reshape(depth, n_b, 2, N_HEADS, HEAD_DIM, t).transpose(0, 1, 5, 2, 3, 4)

    def kv_sample(parts):
        return jnp.stack(parts, 0).reshape(depth, 1, 2, N_HEADS, HEAD_DIM, n_s).transpose(0, 5, 1, 2, 3, 4)

    return (y_prompt, y_sample, kv_prompt(outs["sb_p"]), kv_prompt(outs["mb_p"]),
            kv_sample(outs["sb_s"]), kv_sample(outs["mb_s"]))
```

```python
import functools

import jax
import jax.numpy as jnp
from jax import lax
from jax.experimental import pallas as pl
from jax.experimental.pallas import tpu as pltpu

HEAD_DIM = 64
N_HEADS = 8
WIDTH = N_HEADS * HEAD_DIM
MOBA_BLOCK = 256
MOBA_TOPK = 3
RMS_EPS = 1e-6
ATT_SCALE = HEAD_DIM ** -0.5
ALIBI_SLOPES = tuple(2.0 ** (-8.0 * (h + 1) / N_HEADS) for h in range(N_HEADS))
NEG = -1e30
SB_DEAD = -104.0
SB_TAIL_KEYS = 256
MXU_DTYPE = jnp.bfloat16
F32 = jnp.float32
VMEM_LIMIT_BYTES = 56 * 1024 * 1024
NT_DIMS = (((1,), (1,)), ((), ()))


def _params(*semantics):
    return pltpu.CompilerParams(dimension_semantics=semantics, vmem_limit_bytes=VMEM_LIMIT_BYTES)


def _mm(a, b):
    return jnp.dot(a, b, preferred_element_type=F32)


def _mm_nt(a, b):
    return lax.dot_general(a, b, NT_DIMS, preferred_element_type=F32)


def _neg_softplus(z):
    return -(jnp.maximum(z, 0.0) + jnp.log(1.0 + jnp.exp(-jnp.abs(z))))


def _split_hi_lo(x):
    hi = x.astype(MXU_DTYPE)
    lo = (x - hi.astype(F32)).astype(MXU_DTYPE)
    return hi, lo


def _rms_modulate(x, g, shift, scale):
    y = x * lax.rsqrt(jnp.mean(x * x, axis=-1, keepdims=True) + RMS_EPS)
    return (y * g) * (1.0 + scale) + shift


def _ada_kernel(c_ref, w_ref, b_ref, o_ref):
    c = c_ref[...]
    a = (c * jax.nn.sigmoid(c)).astype(MXU_DTYPE)
    o_ref[...] = _mm(a, w_ref[...].astype(MXU_DTYPE)) + b_ref[...]


def _ada_call(c_all, w_ada, b_ada, tn=1536):
    depth, d, n = w_ada.shape
    m = c_all.shape[0]
    return pl.pallas_call(
        _ada_kernel,
        out_shape=jax.ShapeDtypeStruct((depth, m, n), F32),
        grid=(depth, n // tn),
        in_specs=[
            pl.BlockSpec((m, d), lambda l, j: (0, 0)),
            pl.BlockSpec((None, d, tn), lambda l, j: (l, 0, j)),
            pl.BlockSpec((None, 1, tn), lambda l, j: (l, 0, j)),
        ],
        out_specs=pl.BlockSpec((None, m, tn), lambda l, j: (l, 0, j)),
        compiler_params=_params("arbitrary", "arbitrary"),
        name="adaln",
    )(c_all, w_ada, b_ada.reshape(depth, 1, n))


def _inproj_kernel(x_ref, g_ref, sh_ref, sc_ref, wq_ref, wkvt_ref, wg_ref, qs_ref, qm_ref, kvts_ref, kvtm_ref,
                   gate_ref):
    hb = _rms_modulate(x_ref[...], g_ref[...], sh_ref[...], sc_ref[...]).astype(MXU_DTYPE)
    w = WIDTH
    qs_ref[...] = (_mm(hb, wq_ref[:, :w]) * ATT_SCALE).astype(qs_ref.dtype)
    qm_ref[...] = (_mm(hb, wq_ref[:, w:]) * ATT_SCALE).astype(qm_ref.dtype)
    kvts_ref[...] = _mm_nt(wkvt_ref[:2 * w, :], hb)
    kvtm_ref[...] = _mm_nt(wkvt_ref[2 * w:, :], hb)
    gate_ref[...] = _mm(hb, wg_ref[...])


def _row_spec(tm, d):
    return pl.BlockSpec((None, tm, d), lambda b, i: (b, i, 0))


def _mod_spec(mod, tm, d):
    if mod.shape[1] == 1:
        return pl.BlockSpec((None, 1, d), lambda b, i: (b, 0, 0))
    return pl.BlockSpec((None, tm, d), lambda b, i: (b, i, 0))


def _full_spec(a):
    nd = a.ndim
    return pl.BlockSpec(a.shape, lambda b, i: (0,) * nd)


def _inproj_call(x, g, shift, scale, w_q, w_kvt, w_g, tm):
    b, t, d = x.shape
    row = lambda width, dtype: jax.ShapeDtypeStruct((b, t, width), dtype)
    kvt = jax.ShapeDtypeStruct((b, 2 * WIDTH, t), F32)
    kvt_spec = pl.BlockSpec((None, 2 * WIDTH, tm), lambda bb, i: (bb, 0, i))
    return pl.pallas_call(
        _inproj_kernel,
        out_shape=(row(WIDTH, MXU_DTYPE), row(WIDTH, MXU_DTYPE), kvt, kvt, row(w_g.shape[-1], F32)),
        grid=(b, t // tm),
        in_specs=[_row_spec(tm, d), _full_spec(g), _mod_spec(shift, tm, d), _mod_spec(scale, tm, d),
                  _full_spec(w_q), _full_spec(w_kvt), _full_spec(w_g)],
        out_specs=(_row_spec(tm, WIDTH), _row_spec(tm, WIDTH), kvt_spec, kvt_spec, _row_spec(tm, w_g.shape[-1])),
        compiler_params=_params("arbitrary", "arbitrary"),
        name="inproj",
    )(x, g, shift, scale, w_q, w_kvt, w_g)


def _stage_kv(kvt_ref, kb_ref, vt_ref, tile, km_ref=None):
    for c in range(kb_ref.shape[0]):
        k_rows = kvt_ref[:WIDTH, c * tile:(c + 1) * tile].T
        kb_ref[c] = k_rows.astype(MXU_DTYPE)
        vt_ref[c] = kvt_ref[WIDTH:, c * tile:(c + 1) * tile].astype(MXU_DTYPE)
        if km_ref is not None:
            km_ref[c:c + 1, :] = (jnp.sum(k_rows, axis=0, keepdims=True) * (1.0 / tile)).astype(MXU_DTYPE)


def _sb_prompt_kernel(q_ref, kv_ref, o_ref, kb_ref, vt_ref, carry_ref, ot_ref, *, tile):
    i = pl.program_id(1)

    @pl.when(i == 0)
    def _():
        _stage_kv(kv_ref, kb_ref, vt_ref, tile)

    row = lax.broadcasted_iota(jnp.int32, (tile, tile), 0)
    col = lax.broadcasted_iota(jnp.int32, (tile, tile), 1)
    before = row < col
    later = jnp.where(before, 1.0, 0.0).astype(MXU_DTYPE)

    heads = [slice(h * HEAD_DIM, (h + 1) * HEAD_DIM) for h in range(N_HEADS)]

    def key_tile(j, diagonal):
        zs = [_mm_nt(kb_ref[j, :, hs], q_ref[:, hs]) for hs in heads]
        lfs, sums = [], []
        for z in zs:
            lf = _neg_softplus(z)
            lfs.append(lf)
            hi, lo = _split_hi_lo(jnp.where(before, lf, 0.0) if diagonal else lf)
            sums.append(_mm(later, hi) + _mm(later, lo))
        for h, (hs, z, lf, sm) in enumerate(zip(heads, zs, lfs, sums)):
            total = jnp.sum(jnp.where(before, lf, 0.0) if diagonal else lf, axis=0, keepdims=True)
            if diagonal:
                a = jnp.where(before, jnp.exp(z + lf + sm), 0.0)
                ot_ref[hs, :] = _mm(vt_ref[j, hs, :], a.astype(MXU_DTYPE))
                carry_ref[h:h + 1, :] = total
            else:
                carry = carry_ref[h:h + 1, :]
                a = jnp.exp(z + lf + (sm + carry))
                ot_ref[hs, :] += _mm(vt_ref[j, hs, :], a.astype(MXU_DTYPE))
                carry_ref[h:h + 1, :] = carry + total

    def alive():
        return (jnp.max(carry_ref[...]) > SB_DEAD).astype(jnp.int32)

    def walk(state):
        j, _ = state
        key_tile(j, False)
        return j - 1, alive()

    key_tile(i, True)
    lax.while_loop(lambda state: jnp.logical_and(state[0] >= 0, state[1] > 0), walk, (i - 1, alive()))
    o_ref[...] = ot_ref[...].T.astype(o_ref.dtype)


def _moba_prompt_kernel(q_ref, kv_ref, o_ref, kb_ref, vt_ref, km_ref, nb_ref, m_ref, l_ref, ot_ref, *, tile):
    i = pl.program_id(1)
    n_blk = kb_ref.shape[0]

    @pl.when(i == 0)
    def _():
        _stage_kv(kv_ref, kb_ref, vt_ref, tile, km_ref)

    row = lax.broadcasted_iota(jnp.int32, (tile, tile), 0)
    col = lax.broadcasted_iota(jnp.int32, (tile, tile), 1)
    rel = (row - col).astype(F32)
    causal = row <= col
    blk = lax.broadcasted_iota(jnp.int32, (n_blk, tile), 0)
    heads = [slice(h * HEAD_DIM, (h + 1) * HEAD_DIM) for h in range(N_HEADS)]
    lanes = [slice(h * tile, (h + 1) * tile) for h in range(N_HEADS)]

    def scores(j, bias):
        return jnp.concatenate([_mm_nt(kb_ref[j, :, hs], q_ref[:, hs]) + bias(h) for h, hs in enumerate(heads)],
                               axis=1)

    for hs, ls in zip(heads, lanes):
        gate = _mm_nt(km_ref[:, hs], q_ref[:, hs])
        rank = jnp.zeros((n_blk, tile), F32)
        for n2 in range(n_blk):
            g2 = gate[n2:n2 + 1, :]
            beats = (g2 > gate) | ((g2 == gate) & (n2 < blk))
            rank = rank + jnp.where(beats, 1.0, 0.0) * jnp.where(n2 < i, 1.0, 0.0)
        chosen = (blk < i) & (rank < MOBA_TOPK)
        nb_ref[:, ls] = jnp.where(chosen, 0.0, NEG)

    s = scores(i, lambda h: jnp.where(causal, ALIBI_SLOPES[h] * rel, NEG))
    m = jnp.max(s, axis=0, keepdims=True)
    p = jnp.exp(s - m)
    m_ref[...] = m
    l_ref[...] = jnp.sum(p, axis=0, keepdims=True)
    pb = p.astype(MXU_DTYPE)
    for hs, ls in zip(heads, lanes):
        ot_ref[hs, :] = _mm(vt_ref[i, hs, :], pb[:, ls])

    def past_block(j, c):
        dist = rel + (tile * (j - i)).astype(F32)
        s = scores(j, lambda h: ALIBI_SLOPES[h] * dist) + nb_ref[pl.ds(j, 1), :]
        m = m_ref[...]
        m_new = jnp.maximum(m, jnp.max(s, axis=0, keepdims=True))
        corr = jnp.exp(m - m_new)
        p = jnp.exp(s - m_new)
        l_ref[...] = l_ref[...] * corr + jnp.sum(p, axis=0, keepdims=True)
        m_ref[...] = m_new
        pb = p.astype(MXU_DTYPE)
        for hs, ls in zip(heads, lanes):
            ot_ref[hs, :] = ot_ref[hs, :] * corr[:, ls] + _mm(vt_ref[j, hs, :], pb[:, ls])
        return c

    lax.fori_loop(0, i, past_block, 0)
    l = l_ref[...]
    for hs, ls in zip(heads, lanes):
        ot_ref[hs, :] = ot_ref[hs, :] / l[:, ls]
    o_ref[...] = ot_ref[...].T.astype(o_ref.dtype)


def _prompt_attention_call(kernel_fn, q, kv, extra_scratch, name):
    b, t, _ = q.shape
    tile = MOBA_BLOCK
    assert t % tile == 0
    n_t = t // tile
    scratch = [pltpu.VMEM((n_t, tile, WIDTH), MXU_DTYPE), pltpu.VMEM((n_t, WIDTH, tile), MXU_DTYPE)]
    scratch += extra_scratch(n_t, tile) + [pltpu.VMEM((WIDTH, tile), F32)]
    return pl.pallas_call(
        functools.partial(kernel_fn, tile=tile),
        out_shape=jax.ShapeDtypeStruct((b, t, WIDTH), MXU_DTYPE),
        grid=(b, n_t),
        in_specs=[pl.BlockSpec((None, tile, WIDTH), lambda bb, i: (bb, i, 0)),
                  pl.BlockSpec((None, 2 * WIDTH, t), lambda bb, i: (bb, 0, 0))],
        out_specs=pl.BlockSpec((None, tile, WIDTH), lambda bb, i: (bb, i, 0)),
        scratch_shapes=scratch,
        compiler_params=_params("arbitrary", "arbitrary"),
        name=name,
    )(q, kv)


def _sb_prompt_call(q, kv):
    extra = lambda n_t, tile: [pltpu.VMEM((N_HEADS, tile), F32)]
    return _prompt_attention_call(_sb_prompt_kernel, q, kv, extra, "sb_prompt")


def _moba_prompt_call(q, kv):
    extra = lambda n_t, tile: [pltpu.VMEM((n_t, WIDTH), MXU_DTYPE), pltpu.VMEM((n_t, N_HEADS * tile), F32),
                               pltpu.VMEM((1, N_HEADS * tile), F32), pltpu.VMEM((1, N_HEADS * tile), F32)]
    return _prompt_attention_call(_moba_prompt_kernel, q, kv, extra, "moba_prompt")


def _decode_kernel(pt_ref, layer_ref, qs_ref, qm_ref, kvtn_ref, sb_hbm_ref, *refs, n_pages, n_tail, page):
    b = pl.program_id(0)
    sb_tail, mb_pages = refs[:n_tail], refs[n_tail:n_tail + n_pages]
    osb_ref, omb_ref, km_ref, kvn_ref, acc_ref, carry_ref, page_ref, page_sem = refs[n_tail + n_pages:]
    past = n_pages * page
    pages_per_blk = MOBA_BLOCK // page
    n_blk = past // MOBA_BLOCK

    @pl.when(b == 0)
    def _():
        kvn_ref[...] = kvtn_ref[...].T

    head_of_lane = lax.broadcasted_iota(jnp.int32, (N_HEADS, WIDTH), 1) // HEAD_DIM
    head_row = lax.broadcasted_iota(jnp.int32, (N_HEADS, WIDTH), 0)
    own_cols = head_of_lane == head_row

    def spread(q_row):
        return jnp.where(own_cols, q_row.astype(F32), 0.0)

    def gather_heads(x):
        return jnp.sum(jnp.where(own_cols, x, 0.0), axis=0, keepdims=True)

    qb = spread(qs_ref[...]).astype(MXU_DTYPE)
    r = lax.broadcasted_iota(jnp.int32, (page, 2 * page), 0)
    c = lax.broadcasted_iota(jnp.int32, (page, 2 * page), 1)
    later_and_total = jnp.where((r > c) | (c >= page), 1.0, 0.0).astype(MXU_DTYPE)

    def sb_pages_terms(page_refs, carry):
        zs = [_mm(qb, ref[:WIDTH, :].astype(MXU_DTYPE)) for ref in page_refs]
        lfs = [_neg_softplus(z) for z in zs]
        sums = []
        for lf in lfs:
            hi, lo = _split_hi_lo(lf)
            sums.append(_mm(hi, later_and_total) + _mm(lo, later_and_total))
        acc = jnp.zeros((N_HEADS, WIDTH), F32)
        for ref, z, lf, sm in zip(page_refs, zs, lfs, sums):
            a = jnp.exp(z + lf + (carry + sm[:, :page]))
            acc = acc + _mm_nt(a.astype(MXU_DTYPE), ref[WIDTH:, :].astype(MXU_DTYPE))
            carry = carry + sm[:, page:]
        return acc, carry

    acc, carry = sb_pages_terms(sb_tail[::-1], jnp.zeros((N_HEADS, page), F32))
    acc_ref[...] = acc
    carry_ref[...] = carry

    def alive():
        return (jnp.max(carry_ref[...]) > SB_DEAD).astype(jnp.int32)

    def older_page(state):
        p, _ = state
        fetch = pltpu.make_async_copy(sb_hbm_ref.at[layer_ref[0], pt_ref[b * n_pages + p]], page_ref, page_sem)
        fetch.start()
        fetch.wait()
        acc, carry = sb_pages_terms([page_ref], carry_ref[...])
        acc_ref[...] += acc
        carry_ref[...] = carry
        return p - 1, alive()

    lax.while_loop(lambda state: jnp.logical_and(state[0] >= 0, state[1] > 0), older_page,
                   (n_pages - n_tail - 1, alive()))
    osb_ref[...] = gather_heads(acc_ref[...]).astype(osb_ref.dtype)

    qb = spread(qm_ref[...]).astype(MXU_DTYPE)
    for n in range(n_blk):
        tot = mb_pages[n * pages_per_blk][:WIDTH, :]
        for p in range(n * pages_per_blk + 1, (n + 1) * pages_per_blk):
            tot = tot + mb_pages[p][:WIDTH, :]
        km_ref[:, n:n + 1] = jnp.sum(tot, axis=1, keepdims=True) * (1.0 / MOBA_BLOCK)
    gate = _mm(qb, km_ref[...].astype(MXU_DTYPE))
    blk = lax.broadcasted_iota(jnp.int32, (N_HEADS, n_blk), 1)
    rank = jnp.zeros((N_HEADS, n_blk), F32)
    for n2 in range(n_blk):
        g2 = gate[:, n2:n2 + 1]
        beats = (g2 > gate) | ((g2 == gate) & (n2 < blk))
        rank = rank + jnp.where(beats, 1.0, 0.0)
    not_chosen = jnp.where(rank < MOBA_TOPK, 0.0, NEG)

    slope = jnp.zeros((N_HEADS, page), F32)
    hrow = lax.broadcasted_iota(jnp.int32, (N_HEADS, page), 0)
    for h in range(N_HEADS):
        slope = jnp.where(hrow == h, ALIBI_SLOPES[h], slope)
    lane = lax.broadcasted_iota(jnp.int32, (N_HEADS, page), 1)

    kv_new = kvn_ref[pl.ds(b, 1), :].astype(MXU_DTYPE).astype(F32)
    s_own = jnp.sum(qb.astype(F32) * kv_new[:, :WIDTH], axis=1, keepdims=True)
    scores = []
    m = s_own
    for p in range(n_pages):
        dist = (past - p * page - lane).astype(F32)
        n = p // pages_per_blk
        s = _mm(qb, mb_pages[p][:WIDTH, :].astype(MXU_DTYPE)) - slope * dist + not_chosen[:, n:n + 1]
        scores.append(s)
        m = jnp.maximum(m, jnp.max(s, axis=1, keepdims=True))
    p_own = jnp.exp(s_own - m)
    l = p_own
    acc = p_own.astype(MXU_DTYPE).astype(F32) * kv_new[:, WIDTH:]
    for p in range(n_pages):
        w = jnp.exp(scores[p] - m)
        l = l + jnp.sum(w, axis=1, keepdims=True)
        acc = acc + _mm_nt(w.astype(MXU_DTYPE), mb_pages[p][WIDTH:, :].astype(MXU_DTYPE))
    omb_ref[...] = gather_heads(acc / l).astype(omb_ref.dtype)


def _decode_call(layer, page_table, q_sb, q_mb, kvt_mb_new, cache_sb, cache_mb):
    n_seq, n_pages = page_table.shape
    page = cache_sb.shape[-1]
    assert (n_pages * page) % MOBA_BLOCK == 0 and MOBA_BLOCK % page == 0
    n_tail = min(n_pages, SB_TAIL_KEYS // page)

    def page_spec(p):
        return pl.BlockSpec((None, None, 2 * WIDTH, page),
                            lambda b, pt, lay: (lay[0], pt[b * n_pages + p], 0, 0))

    def seq_spec(width):
        return pl.BlockSpec((None, 1, width), lambda b, pt, lay: (b, 0, 0))

    grid_spec = pltpu.PrefetchScalarGridSpec(
        num_scalar_prefetch=2,
        grid=(n_seq,),
        in_specs=[seq_spec(WIDTH), seq_spec(WIDTH), pl.BlockSpec((2 * WIDTH, n_seq), lambda b, pt, lay: (0, 0)),
                  pl.BlockSpec(memory_space=pl.ANY)]
        + [page_spec(p) for p in range(n_pages - n_tail, n_pages)] + [page_spec(p) for p in range(n_pages)],
        out_specs=(seq_spec(WIDTH), seq_spec(WIDTH)),
        scratch_shapes=[pltpu.VMEM((WIDTH, n_pages * page // MOBA_BLOCK), F32),
                        pltpu.VMEM((n_seq, 2 * WIDTH), F32),
                        pltpu.VMEM((N_HEADS, WIDTH), F32),
                        pltpu.VMEM((N_HEADS, page), F32),
                        pltpu.VMEM((2 * WIDTH, page), F32),
                        pltpu.SemaphoreType.DMA(())],
    )
    out_shape = (jax.ShapeDtypeStruct((n_seq, 1, WIDTH), MXU_DTYPE),) * 2
    return pl.pallas_call(
        functools.partial(_decode_kernel, n_pages=n_pages, n_tail=n_tail, page=page),
        out_shape=out_shape,
        grid_spec=grid_spec,
        compiler_params=_params("arbitrary"),
        name="decode_attention",
    )(page_table.reshape(-1), jnp.full((1,), layer, jnp.int32), q_sb, q_mb, kvt_mb_new, cache_sb,
      *([cache_sb] * n_tail), *([cache_mb] * n_pages))


def _merge_kernel(osb_ref, omb_ref, gate_ref, x_ref, ga_ref, wps_ref, wpm_ref, wo_ref, xo_ref):
    d = x_ref.shape[-1]
    merged = (jax.nn.sigmoid(gate_ref[:, :d]) * _mm(osb_ref[...], wps_ref[...])
              + jax.nn.sigmoid(gate_ref[:, d:]) * _mm(omb_ref[...], wpm_ref[...]))
    xo_ref[...] = x_ref[...] + ga_ref[...] * _mm(merged.astype(MXU_DTYPE), wo_ref[...])


def _merge_call(o_sb, o_mb, gates, x, ga1, w_ps, w_pm, w_o, tm):
    b, t, d = x.shape
    return pl.pallas_call(
        _merge_kernel,
        out_shape=jax.ShapeDtypeStruct(x.shape, F32),
        grid=(b, t // tm),
        in_specs=[_row_spec(tm, WIDTH), _row_spec(tm, WIDTH), _row_spec(tm, gates.shape[-1]), _row_spec(tm, d),
                  _mod_spec(ga1, tm, d), _full_spec(w_ps), _full_spec(w_pm), _full_spec(w_o)],
        out_specs=_row_spec(tm, d),
        compiler_params=_params("arbitrary", "arbitrary"),
        name="merge_outproj",
    )(o_sb, o_mb, gates, x, ga1, w_ps, w_pm, w_o)


def _ffn_kernel(x_ref, g_ref, sh_ref, sc_ref, ga_ref, wgu_ref, wdn_ref, xo_ref, acc_ref):
    x = x_ref[...]
    hb = _rms_modulate(x, g_ref[...], sh_ref[...], sc_ref[...]).astype(MXU_DTYPE)
    n_chunks = wdn_ref.shape[0]
    for c in range(n_chunks):
        gate = _mm(hb, wgu_ref[c])
        up = _mm(hb, wgu_ref[n_chunks + c])
        act = (gate * jax.nn.sigmoid(gate) * up).astype(MXU_DTYPE)
        part = _mm(act, wdn_ref[c])
        if c == 0:
            acc_ref[...] = part
        else:
            acc_ref[...] += part
    xo_ref[...] = x + ga_ref[...] * acc_ref[...]


def _ffn_call(x, g, shift, scale, ga2, w_gu, w_dn, tm):
    b, t, d = x.shape
    return pl.pallas_call(
        _ffn_kernel,
        out_shape=jax.ShapeDtypeStruct(x.shape, F32),
        grid=(b, t // tm),
        in_specs=[_row_spec(tm, d), _full_spec(g), _mod_spec(shift, tm, d), _mod_spec(scale, tm, d),
                  _mod_spec(ga2, tm, d), _full_spec(w_gu), _full_spec(w_dn)],
        out_specs=_row_spec(tm, d),
        scratch_shapes=[pltpu.VMEM((tm, d), F32)],
        compiler_params=_params("arbitrary", "arbitrary"),
        name="swiglu",
    )(x, g, shift, scale, ga2, w_gu, w_dn)


def _final_norm_kernel(x_ref, g_ref, o_ref):
    x = x_ref[...]
    o_ref[...] = x * lax.rsqrt(jnp.mean(x * x, axis=-1, keepdims=True) + RMS_EPS) * g_ref[...]


def _final_norm_call(x, g, tm):
    b, t, d = x.shape
    return pl.pallas_call(
        _final_norm_kernel,
        out_shape=jax.ShapeDtypeStruct(x.shape, F32),
        grid=(b, t // tm),
        in_specs=[_row_spec(tm, d), _full_spec(g)],
        out_specs=_row_spec(tm, d),
        compiler_params=_params("arbitrary", "arbitrary"),
        name="final_norm",
    )(x, g)


FFN_CHUNK = 256


def _row_tile(t):
    for tm in (512, 256, 128, 64, 32, 16, 8):
        if t % tm == 0:
            return tm
    return t


def kernel(x_prompt, x_sample, c_prompt, c_sample, cache_sb_kv, cache_moba_kv, page_table, w_ada, b_ada, g_mix,
           w_in, w_proj_sb, w_proj_moba, w_out, g_ffn, w_gate_up, w_down, g_final):
    n_b, t, d = x_prompt.shape
    n_s = x_sample.shape[0]
    depth, n_pool, page = cache_sb_kv.shape[:3]
    d_ff = w_down.shape[1]
    assert x_sample.shape[1] == 1 and d_ff % FFN_CHUNK == 0
    n_chunks = d_ff // FFN_CHUNK
    w = WIDTH

    w_in_b = w_in.astype(MXU_DTYPE)
    w_q = jnp.concatenate([w_in_b[:, :, 0:w], w_in_b[:, :, 3 * w:4 * w]], axis=2)
    w_kvt = jnp.concatenate([w_in_b[:, :, w:3 * w], w_in_b[:, :, 4 * w:6 * w]], axis=2).transpose(0, 2, 1)
    w_g = w_in_b[:, :, 6 * w:]
    w_ps_b = w_proj_sb.astype(MXU_DTYPE)
    w_pm_b = w_proj_moba.astype(MXU_DTYPE)
    w_o_b = w_out.astype(MXU_DTYPE)
    w_gu_b = (w_gate_up.astype(MXU_DTYPE).reshape(depth, d, 2 * n_chunks, FFN_CHUNK).transpose(0, 2, 1, 3))
    w_dn_b = w_down.astype(MXU_DTYPE).reshape(depth, n_chunks, FFN_CHUNK, d)

    to_pages = lambda cache: cache.transpose(0, 1, 3, 4, 5, 2).reshape(depth, n_pool, 2 * w, page)
    cache_sb_t, cache_mb_t = to_pages(cache_sb_kv), to_pages(cache_moba_kv)

    mod = _ada_call(jnp.concatenate([c_prompt, c_sample], axis=0), w_ada, b_ada)

    xp = x_prompt
    xs = x_sample.reshape(1, n_s, d)
    tm_p, tm_s = _row_tile(t), _row_tile(n_s)
    outs = {k: [] for k in ("sb_p", "mb_p", "sb_s", "mb_s")}
    for l in range(depth):
        mod_p = [mod[l, :n_b, k * d:(k + 1) * d].reshape(n_b, 1, d) for k in range(6)]
        mod_s = [mod[l, n_b:, k * d:(k + 1) * d].reshape(1, n_s, d) for k in range(6)]
        g_mix_l, g_ffn_l = g_mix[l].reshape(1, d), g_ffn[l].reshape(1, d)

        qs, qm, kvts, kvtm, gates = _inproj_call(xp, g_mix_l, mod_p[0], mod_p[1], w_q[l], w_kvt[l], w_g[l], tm_p)
        o_sb = _sb_prompt_call(qs, kvts)
        o_mb = _moba_prompt_call(qm, kvtm)
        xp = _merge_call(o_sb, o_mb, gates, xp, mod_p[2], w_ps_b[l], w_pm_b[l], w_o_b[l], tm_p)
        xp = _ffn_call(xp, g_ffn_l, mod_p[3], mod_p[4], mod_p[5], w_gu_b[l], w_dn_b[l], tm_p)
        outs["sb_p"].append(kvts)
        outs["mb_p"].append(kvtm)

        qs, qm, kvts, kvtm, gates = _inproj_call(xs, g_mix_l, mod_s[0], mod_s[1], w_q[l], w_kvt[l], w_g[l], tm_s)
        o_sb, o_mb = _decode_call(l, page_table, qs.reshape(n_s, 1, w), qm.reshape(n_s, 1, w), kvtm[0],
                                  cache_sb_t, cache_mb_t)
        xs = _merge_call(o_sb.reshape(1, n_s, w), o_mb.reshape(1, n_s, w), gates, xs, mod_s[2],
                         w_ps_b[l], w_pm_b[l], w_o_b[l], tm_s)
        xs = _ffn_call(xs, g_ffn_l, mod_s[3], mod_s[4], mod_s[5], w_gu_b[l], w_dn_b[l], tm_s)
        outs["sb_s"].append(kvts)
        outs["mb_s"].append(kvtm)

    g_fin = g_final.reshape(1, d)
    y_prompt = _final_norm_call(xp, g_fin, tm_p)
    y_sample = _final_norm_call(xs, g_fin, tm_s).reshape(n_s, 1, d)

    def kv_prompt(parts):
        return jnp.stack(parts, 0).reshape(depth, n_b, 2, N_HEADS, HEAD_DIM, t).transpose(0, 1, 5, 2, 3, 4)

    def kv_sample(parts):
        return jnp.stack(parts, 0).reshape(depth, 1, 2, N_HEADS, HEAD_DIM, n_s).transpose(0, 5, 1, 2, 3, 4)

    return (y_prompt, y_sample, kv_prompt(outs["sb_p"]), kv_prompt(outs["mb_p"]),
            kv_sample(outs["sb_s"]), kv_sample(outs["mb_s"]))
```

```python
import functools

import jax
import jax.numpy as jnp
from jax import lax
from jax.experimental import pallas as pl
from jax.experimental.pallas import tpu as pltpu

HEAD_DIM = 64
N_HEADS = 8
WIDTH = N_HEADS * HEAD_DIM
MOBA_BLOCK = 256
MOBA_TOPK = 3
RMS_EPS = 1e-6
ATT_SCALE = HEAD_DIM ** -0.5
ALIBI_SLOPES = tuple(2.0 ** (-8.0 * (h + 1) / N_HEADS) for h in range(N_HEADS))
NEG = -1e30
SB_DEAD = -104.0
SB_TAIL_KEYS = 256
SB_TILE = 256
MXU_DTYPE = jnp.bfloat16
F32 = jnp.float32
VMEM_LIMIT_BYTES = 56 * 1024 * 1024
NT_DIMS = (((1,), (1,)), ((), ()))


def _params(*semantics):
    return pltpu.CompilerParams(dimension_semantics=semantics, vmem_limit_bytes=VMEM_LIMIT_BYTES)


def _mm(a, b):
    return jnp.dot(a, b, preferred_element_type=F32)


def _mm_nt(a, b):
    return lax.dot_general(a, b, NT_DIMS, preferred_element_type=F32)


def _neg_softplus(z):
    return -(jnp.maximum(z, 0.0) + jnp.log(1.0 + jnp.exp(-jnp.abs(z))))


def _split_hi_lo(x):
    hi = x.astype(MXU_DTYPE)
    lo = (x - hi.astype(F32)).astype(MXU_DTYPE)
    return hi, lo


def _rms_modulate(x, g, shift, scale):
    y = x * lax.rsqrt(jnp.mean(x * x, axis=-1, keepdims=True) + RMS_EPS)
    return (y * g) * (1.0 + scale) + shift


def _ada_kernel(c_ref, w_ref, b_ref, o_ref):
    c = c_ref[...]
    a = (c * jax.nn.sigmoid(c)).astype(MXU_DTYPE)
    o_ref[...] = _mm(a, w_ref[...].astype(MXU_DTYPE)) + b_ref[...]


def _ada_call(c_all, w_ada, b_ada, tn=1536):
    depth, d, n = w_ada.shape
    m = c_all.shape[0]
    return pl.pallas_call(
        _ada_kernel,
        out_shape=jax.ShapeDtypeStruct((depth, m, n), F32),
        grid=(depth, n // tn),
        in_specs=[
            pl.BlockSpec((m, d), lambda l, j: (0, 0)),
            pl.BlockSpec((None, d, tn), lambda l, j: (l, 0, j)),
            pl.BlockSpec((None, 1, tn), lambda l, j: (l, 0, j)),
        ],
        out_specs=pl.BlockSpec((None, m, tn), lambda l, j: (l, 0, j)),
        compiler_params=_params("arbitrary", "arbitrary"),
        name="adaln",
    )(c_all, w_ada, b_ada.reshape(depth, 1, n))


def _inproj_kernel(x_ref, g_ref, sh_ref, sc_ref, wq_ref, wkvt_ref, wg_ref, *refs):
    qs_ref, qm_ref, kvts_ref, kvtm_ref, gate_ref = refs[-5:]
    hb = _rms_modulate(x_ref[...], g_ref[...], sh_ref[...], sc_ref[...]).astype(MXU_DTYPE)
    w = WIDTH
    qs_ref[...] = (_mm(hb, wq_ref[:, :w]) * ATT_SCALE).astype(qs_ref.dtype)
    qm_ref[...] = (_mm(hb, wq_ref[:, w:]) * ATT_SCALE).astype(qm_ref.dtype)
    kvts_ref[...] = _mm_nt(wkvt_ref[:2 * w, :], hb)
    kvtm_ref[...] = _mm_nt(wkvt_ref[2 * w:, :], hb)
    gate_ref[...] = _mm(hb, wg_ref[...])


def _row_spec(tm, d):
    return pl.BlockSpec((None, tm, d), lambda b, i: (b, i, 0))


def _mod_spec(mod, tm, d):
    if mod.shape[1] == 1:
        return pl.BlockSpec((None, 1, d), lambda b, i: (b, 0, 0))
    return pl.BlockSpec((None, tm, d), lambda b, i: (b, i, 0))


def _full_spec(a):
    nd = a.ndim
    return pl.BlockSpec(a.shape, lambda b, i: (0,) * nd)


def _inproj_call(x, g, shift, scale, w_q, w_kvt, w_g, tm, layer, depth, kv_stacks):
    b, t, d = x.shape
    row = lambda width, dtype: jax.ShapeDtypeStruct((b, t, width), dtype)
    kvt = jax.ShapeDtypeStruct((depth, b, 2 * WIDTH, t), F32)
    kvt_spec = pl.BlockSpec((None, None, 2 * WIDTH, tm), lambda bb, i: (layer, bb, 0, i))
    operands = [x, g, shift, scale, w_q, w_kvt, w_g]
    in_specs = [_row_spec(tm, d), _full_spec(g), _mod_spec(shift, tm, d), _mod_spec(scale, tm, d),
                _full_spec(w_q), _full_spec(w_kvt), _full_spec(w_g)]
    aliases = {len(operands): 2, len(operands) + 1: 3}
    operands += list(kv_stacks)
    in_specs += [pl.BlockSpec(memory_space=pl.ANY)] * 2
    return pl.pallas_call(
        _inproj_kernel,
        out_shape=(row(WIDTH, MXU_DTYPE), row(WIDTH, MXU_DTYPE), kvt, kvt, row(w_g.shape[-1], F32)),
        grid=(b, t // tm),
        in_specs=in_specs,
        out_specs=(_row_spec(tm, WIDTH), _row_spec(tm, WIDTH), kvt_spec, kvt_spec, _row_spec(tm, w_g.shape[-1])),
        input_output_aliases=aliases,
        compiler_params=_params("arbitrary", "arbitrary"),
        name="inproj",
    )(*operands)


LANES = 128
HEAD_PAIRS = [slice((h // 2) * LANES, (h // 2 + 1) * LANES) for h in range(N_HEADS)]


def _own_lanes(h, rows):
    lane = lax.broadcasted_iota(jnp.int32, (rows, LANES), 1)
    return lane < HEAD_DIM if h % 2 == 0 else lane >= HEAD_DIM


def _spare_lane(h, rows, k):
    lane = lax.broadcasted_iota(jnp.int32, (rows, LANES), 1)
    return lane == (HEAD_DIM + k if h % 2 == 0 else k)


def _stage_kv(kvt_ref, ka_ref, vt_ref, tile, km_ref=None, key_extras=None):
    for c in range(ka_ref.shape[0]):
        k_rows = kvt_ref[:WIDTH, c * tile:(c + 1) * tile].T
        vt_ref[c] = kvt_ref[WIDTH:, c * tile:(c + 1) * tile].astype(MXU_DTYPE)
        for h in range(N_HEADS):
            other = 0.0 if key_extras is None else key_extras(h)
            ka_ref[c, h] = jnp.where(_own_lanes(h, tile), k_rows[:, HEAD_PAIRS[h]], other).astype(MXU_DTYPE)
        if km_ref is not None:
            km_ref[c:c + 1, :] = (jnp.sum(k_rows, axis=0, keepdims=True) * (1.0 / tile)).astype(MXU_DTYPE)


def _sb_prompt_kernel(q_ref, kv_ref, o_ref, kb_ref, vt_ref, carry_ref, ot_ref, *, tile):
    i = pl.program_id(1)

    @pl.when(i == 0)
    def _():
        _stage_kv(kv_ref, kb_ref, vt_ref, tile)

    row = lax.broadcasted_iota(jnp.int32, (tile, tile), 0)
    col = lax.broadcasted_iota(jnp.int32, (tile, tile), 1)
    before = row < col
    later = jnp.where(before, 1.0, 0.0).astype(MXU_DTYPE)

    heads = [slice(h * HEAD_DIM, (h + 1) * HEAD_DIM) for h in range(N_HEADS)]

    def key_tile(j, diagonal):
        zs = [_mm_nt(kb_ref[j, h], q_ref[:, HEAD_PAIRS[h]]) for h in range(N_HEADS)]
        lfs, sums = [], []
        for z in zs:
            lf = _neg_softplus(z)
            lfs.append(lf)
            hi, lo = _split_hi_lo(jnp.where(before, lf, 0.0) if diagonal else lf)
            sums.append(_mm(later, hi) + _mm(later, lo))
        for h, (hs, z, lf, sm) in enumerate(zip(heads, zs, lfs, sums)):
            total = jnp.sum(jnp.where(before, lf, 0.0) if diagonal else lf, axis=0, keepdims=True)
            if diagonal:
                a = jnp.where(before, jnp.exp(z + lf + sm), 0.0)
                ot_ref[hs, :] = _mm(vt_ref[j, hs, :], a.astype(MXU_DTYPE))
                carry_ref[h:h + 1, :] = total
            else:
                carry = carry_ref[h:h + 1, :]
                a = jnp.exp(z + lf + (sm + carry))
                ot_ref[hs, :] += _mm(vt_ref[j, hs, :], a.astype(MXU_DTYPE))
                carry_ref[h:h + 1, :] = carry + total

    def alive():
        return (jnp.max(carry_ref[...]) > SB_DEAD).astype(jnp.int32)

    def walk(state):
        j, _ = state
        key_tile(j, False)
        return j - 1, alive()

    key_tile(i, True)
    lax.while_loop(lambda state: jnp.logical_and(state[0] >= 0, state[1] > 0), walk, (i - 1, alive()))
    o_ref[...] = ot_ref[...].T.astype(o_ref.dtype)


def _moba_prompt_kernel(q_ref, kv_ref, o_ref, ka_ref, vt_ref, km_ref, nb_ref, qa_ref, m_ref, l_ref, ot_ref, *,
                        tile):
    i = pl.program_id(1)
    n_blk = ka_ref.shape[0]
    assert tile <= 256
    tile_row = lax.broadcasted_iota(jnp.int32, (tile, LANES), 0).astype(F32)

    def key_extras(h):
        return jnp.where(_spare_lane(h, tile, 0), tile_row, jnp.where(_spare_lane(h, tile, 1), 1.0, 0.0))

    @pl.when(i == 0)
    def _():
        _stage_kv(kv_ref, ka_ref, vt_ref, tile, km_ref, key_extras)

    for h in range(N_HEADS):
        slope = ALIBI_SLOPES[h]
        extras = jnp.where(_spare_lane(h, tile, 0), slope,
                           jnp.where(_spare_lane(h, tile, 1), -slope * tile_row, 0.0))
        qa_ref[h] = jnp.where(_own_lanes(h, tile), q_ref[:, HEAD_PAIRS[h]].astype(F32), extras).astype(MXU_DTYPE)

    row = lax.broadcasted_iota(jnp.int32, (tile, tile), 0)
    col = lax.broadcasted_iota(jnp.int32, (tile, tile), 1)
    causal = row <= col
    blk = lax.broadcasted_iota(jnp.int32, (n_blk, tile), 0)
    heads = [slice(h * HEAD_DIM, (h + 1) * HEAD_DIM) for h in range(N_HEADS)]
    lanes = [slice(h * tile, (h + 1) * tile) for h in range(N_HEADS)]
    slope_lanes = jnp.concatenate([jnp.full((1, tile), s, F32) for s in ALIBI_SLOPES], axis=1)

    def scores(j, fix=lambda s: s):
        return jnp.concatenate([fix(_mm_nt(ka_ref[j, h], qa_ref[h])) for h in range(N_HEADS)], axis=1)

    for h, ls in enumerate(lanes):
        km_h = jnp.where(_own_lanes(h, n_blk), km_ref[:, HEAD_PAIRS[h]], jnp.zeros((), MXU_DTYPE))
        gate = _mm_nt(km_h, q_ref[:, HEAD_PAIRS[h]])
        rank = jnp.zeros((n_blk, tile), F32)
        for n2 in range(n_blk):
            g2 = gate[n2:n2 + 1, :]
            beats = (g2 > gate) | ((g2 == gate) & (n2 < blk))
            rank = rank + jnp.where(beats, 1.0, 0.0) * jnp.where(n2 < i, 1.0, 0.0)
        chosen = (blk < i) & (rank < MOBA_TOPK)
        nb_ref[:, ls] = jnp.where(chosen, 0.0, NEG)

    s = scores(i, lambda s: jnp.where(causal, s, NEG))
    m = jnp.max(s, axis=0, keepdims=True)
    p = jnp.exp(s - m)
    m_ref[...] = m
    l_ref[...] = jnp.sum(p, axis=0, keepdims=True)
    pb = p.astype(MXU_DTYPE)
    for hs, ls in zip(heads, lanes):
        ot_ref[hs, :] = _mm(vt_ref[i, hs, :], pb[:, ls])

    def past_block(j, c):
        s = scores(j) + (nb_ref[pl.ds(j, 1), :] + slope_lanes * (tile * (j - i)).astype(F32))
        m = m_ref[...]
        m_new = jnp.maximum(m, jnp.max(s, axis=0, keepdims=True))
        corr = jnp.exp(m - m_new)
        p = jnp.exp(s - m_new)
        l_ref[...] = l_ref[...] * corr + jnp.sum(p, axis=0, keepdims=True)
        m_ref[...] = m_new
        pb = p.astype(MXU_DTYPE)
        for hs, ls in zip(heads, lanes):
            ot_ref[hs, :] = ot_ref[hs, :] * corr[:, ls] + _mm(vt_ref[j, hs, :], pb[:, ls])
        return c

    lax.fori_loop(0, i, past_block, 0)
    l = l_ref[...]
    for hs, ls in zip(heads, lanes):
        ot_ref[hs, :] = ot_ref[hs, :] / l[:, ls]
    o_ref[...] = ot_ref[...].T.astype(o_ref.dtype)


def _prompt_attention_call(kernel_fn, q, kv_stack, layer, tile, extra_scratch, name):
    b, t, _ = q.shape
    assert t % tile == 0
    n_t = t // tile
    scratch = [pltpu.VMEM((n_t, N_HEADS, tile, LANES), MXU_DTYPE), pltpu.VMEM((n_t, WIDTH, tile), MXU_DTYPE)]
    scratch += extra_scratch(n_t, tile) + [pltpu.VMEM((WIDTH, tile), F32)]
    return pl.pallas_call(
        functools.partial(kernel_fn, tile=tile),
        out_shape=jax.ShapeDtypeStruct((b, t, WIDTH), MXU_DTYPE),
        grid=(b, n_t),
        in_specs=[pl.BlockSpec((None, tile, WIDTH), lambda bb, i: (bb, i, 0)),
                  pl.BlockSpec((None, None, 2 * WIDTH, t), lambda bb, i: (layer, bb, 0, 0))],
        out_specs=pl.BlockSpec((None, tile, WIDTH), lambda bb, i: (bb, i, 0)),
        scratch_shapes=scratch,
        compiler_params=_params("arbitrary", "arbitrary"),
        name=name,
    )(q, kv_stack)


def _sb_prompt_call(q, kv_stack, layer):
    extra = lambda n_t, tile: [pltpu.VMEM((N_HEADS, tile), F32)]
    return _prompt_attention_call(_sb_prompt_kernel, q, kv_stack, layer, SB_TILE, extra, "sb_prompt")


def _moba_prompt_call(q, kv_stack, layer):
    extra = lambda n_t, tile: [pltpu.VMEM((n_t, WIDTH), MXU_DTYPE), pltpu.VMEM((n_t, N_HEADS * tile), F32),
                               pltpu.VMEM((N_HEADS, tile, LANES), MXU_DTYPE),
                               pltpu.VMEM((1, N_HEADS * tile), F32), pltpu.VMEM((1, N_HEADS * tile), F32)]
    return _prompt_attention_call(_moba_prompt_kernel, q, kv_stack, layer, MOBA_BLOCK, extra, "moba_prompt")


def _decode_kernel(pt_ref, layer_ref, qs_ref, qm_ref, kvtn_ref, sb_hbm_ref, *refs, n_pages, n_tail, page):
    b = pl.program_id(0)
    sb_tail, mb_pages = refs[:n_tail], refs[n_tail:n_tail + n_pages]
    osb_ref, omb_ref, km_ref, kvn_ref, acc_ref, carry_ref, page_ref, page_sem = refs[n_tail + n_pages:]
    past = n_pages * page
    pages_per_blk = MOBA_BLOCK // page
    n_blk = past // MOBA_BLOCK

    @pl.when(b == 0)
    def _():
        kvn_ref[...] = kvtn_ref[...].T

    head_of_lane = lax.broadcasted_iota(jnp.int32, (N_HEADS, WIDTH), 1) // HEAD_DIM
    head_row = lax.broadcasted_iota(jnp.int32, (N_HEADS, WIDTH), 0)
    own_cols = head_of_lane == head_row

    def spread(q_row):
        return jnp.where(own_cols, q_row.astype(F32), 0.0)

    def gather_heads(x):
        return jnp.sum(jnp.where(own_cols, x, 0.0), axis=0, keepdims=True)

    qb = spread(qs_ref[...]).astype(MXU_DTYPE)
    r = lax.broadcasted_iota(jnp.int32, (page, 2 * page), 0)
    c = lax.broadcasted_iota(jnp.int32, (page, 2 * page), 1)
    later_and_total = jnp.where((r > c) | (c >= page), 1.0, 0.0).astype(MXU_DTYPE)

    def sb_pages_terms(page_refs, carry):
        zs = [_mm(qb, ref[:WIDTH, :].astype(MXU_DTYPE)) for ref in page_refs]
        lfs = [_neg_softplus(z) for z in zs]
        sums = []
        for lf in lfs:
            hi, lo = _split_hi_lo(lf)
            sums.append(_mm(hi, later_and_total) + _mm(lo, later_and_total))
        acc = jnp.zeros((N_HEADS, WIDTH), F32)
        for ref, z, lf, sm in zip(page_refs, zs, lfs, sums):
            a = jnp.exp(z + lf + (carry + sm[:, :page]))
            acc = acc + _mm_nt(a.astype(MXU_DTYPE), ref[WIDTH:, :].astype(MXU_DTYPE))
            carry = carry + sm[:, page:]
        return acc, carry

    acc, carry = sb_pages_terms(sb_tail[::-1], jnp.zeros((N_HEADS, page), F32))
    acc_ref[...] = acc
    carry_ref[...] = carry

    def alive():
        return (jnp.max(carry_ref[...]) > SB_DEAD).astype(jnp.int32)

    def older_page(state):
        p, _ = state
        fetch = pltpu.make_async_copy(sb_hbm_ref.at[layer_ref[0], pt_ref[b * n_pages + p]], page_ref, page_sem)
        fetch.start()
        fetch.wait()
        acc, carry = sb_pages_terms([page_ref], carry_ref[...])
        acc_ref[...] += acc
        carry_ref[...] = carry
        return p - 1, alive()

    lax.while_loop(lambda state: jnp.logical_and(state[0] >= 0, state[1] > 0), older_page,
                   (n_pages - n_tail - 1, alive()))
    osb_ref[...] = gather_heads(acc_ref[...]).astype(osb_ref.dtype)

    qb = spread(qm_ref[...]).astype(MXU_DTYPE)
    for n in range(n_blk):
        tot = mb_pages[n * pages_per_blk][:WIDTH, :]
        for p in range(n * pages_per_blk + 1, (n + 1) * pages_per_blk):
            tot = tot + mb_pages[p][:WIDTH, :]
        km_ref[:, n:n + 1] = jnp.sum(tot, axis=1, keepdims=True) * (1.0 / MOBA_BLOCK)
    gate = _mm(qb, km_ref[...].astype(MXU_DTYPE))
    blk = lax.broadcasted_iota(jnp.int32, (N_HEADS, n_blk), 1)
    rank = jnp.zeros((N_HEADS, n_blk), F32)
    for n2 in range(n_blk):
        g2 = gate[:, n2:n2 + 1]
        beats = (g2 > gate) | ((g2 == gate) & (n2 < blk))
        rank = rank + jnp.where(beats, 1.0, 0.0)
    not_chosen = jnp.where(rank < MOBA_TOPK, 0.0, NEG)

    slope = jnp.zeros((N_HEADS, page), F32)
    hrow = lax.broadcasted_iota(jnp.int32, (N_HEADS, page), 0)
    for h in range(N_HEADS):
        slope = jnp.where(hrow == h, ALIBI_SLOPES[h], slope)
    lane = lax.broadcasted_iota(jnp.int32, (N_HEADS, page), 1)

    kv_new = kvn_ref[pl.ds(b, 1), :].astype(MXU_DTYPE).astype(F32)
    s_own = jnp.sum(qb.astype(F32) * kv_new[:, :WIDTH], axis=1, keepdims=True)
    scores = []
    m = s_own
    for p in range(n_pages):
        dist = (past - p * page - lane).astype(F32)
        n = p // pages_per_blk
        s = _mm(qb, mb_pages[p][:WIDTH, :].astype(MXU_DTYPE)) - slope * dist + not_chosen[:, n:n + 1]
        scores.append(s)
        m = jnp.maximum(m, jnp.max(s, axis=1, keepdims=True))
    p_own = jnp.exp(s_own - m)
    l = p_own
    acc = p_own.astype(MXU_DTYPE).astype(F32) * kv_new[:, WIDTH:]
    for p in range(n_pages):
        w = jnp.exp(scores[p] - m)
        l = l + jnp.sum(w, axis=1, keepdims=True)
        acc = acc + _mm_nt(w.astype(MXU_DTYPE), mb_pages[p][WIDTH:, :].astype(MXU_DTYPE))
    omb_ref[...] = gather_heads(acc / l).astype(omb_ref.dtype)


def _decode_call(layer, page_table, q_sb, q_mb, kvt_mb_new, cache_sb, cache_mb):
    n_seq, n_pages = page_table.shape
    page = cache_sb.shape[-1]
    assert (n_pages * page) % MOBA_BLOCK == 0 and MOBA_BLOCK % page == 0
    n_tail = min(n_pages, SB_TAIL_KEYS // page)

    def page_spec(p):
        return pl.BlockSpec((None, None, 2 * WIDTH, page),
                            lambda b, pt, lay: (lay[0], pt[b * n_pages + p], 0, 0))

    def seq_spec(width):
        return pl.BlockSpec((None, 1, width), lambda b, pt, lay: (b, 0, 0))

    grid_spec = pltpu.PrefetchScalarGridSpec(
        num_scalar_prefetch=2,
        grid=(n_seq,),
        in_specs=[seq_spec(WIDTH), seq_spec(WIDTH), pl.BlockSpec((2 * WIDTH, n_seq), lambda b, pt, lay: (0, 0)),
                  pl.BlockSpec(memory_space=pl.ANY)]
        + [page_spec(p) for p in range(n_pages - n_tail, n_pages)] + [page_spec(p) for p in range(n_pages)],
        out_specs=(seq_spec(WIDTH), seq_spec(WIDTH)),
        scratch_shapes=[pltpu.VMEM((WIDTH, n_pages * page // MOBA_BLOCK), F32),
                        pltpu.VMEM((n_seq, 2 * WIDTH), F32),
                        pltpu.VMEM((N_HEADS, WIDTH), F32),
                        pltpu.VMEM((N_HEADS, page), F32),
                        pltpu.VMEM((2 * WIDTH, page), F32),
                        pltpu.SemaphoreType.DMA(())],
    )
    out_shape = (jax.ShapeDtypeStruct((n_seq, 1, WIDTH), MXU_DTYPE),) * 2
    return pl.pallas_call(
        functools.partial(_decode_kernel, n_pages=n_pages, n_tail=n_tail, page=page),
        out_shape=out_shape,
        grid_spec=grid_spec,
        compiler_params=_params("arbitrary"),
        name="decode_attention",
    )(page_table.reshape(-1), jnp.full((1,), layer, jnp.int32), q_sb, q_mb, kvt_mb_new, cache_sb,
      *([cache_sb] * n_tail), *([cache_mb] * n_pages))


def _merge_kernel(osb_ref, omb_ref, gate_ref, x_ref, ga_ref, wps_ref, wpm_ref, wo_ref, xo_ref):
    d = x_ref.shape[-1]
    merged = (jax.nn.sigmoid(gate_ref[:, :d]) * _mm(osb_ref[...], wps_ref[...])
              + jax.nn.sigmoid(gate_ref[:, d:]) * _mm(omb_ref[...], wpm_ref[...]))
    xo_ref[...] = x_ref[...] + ga_ref[...] * _mm(merged.astype(MXU_DTYPE), wo_ref[...])


def _merge_call(o_sb, o_mb, gates, x, ga1, w_ps, w_pm, w_o, tm):
    b, t, d = x.shape
    return pl.pallas_call(
        _merge_kernel,
        out_shape=jax.ShapeDtypeStruct(x.shape, F32),
        grid=(b, t // tm),
        in_specs=[_row_spec(tm, WIDTH), _row_spec(tm, WIDTH), _row_spec(tm, gates.shape[-1]), _row_spec(tm, d),
                  _mod_spec(ga1, tm, d), _full_spec(w_ps), _full_spec(w_pm), _full_spec(w_o)],
        out_specs=_row_spec(tm, d),
        compiler_params=_params("arbitrary", "arbitrary"),
        name="merge_outproj",
    )(o_sb, o_mb, gates, x, ga1, w_ps, w_pm, w_o)


def _ffn_kernel(x_ref, g_ref, sh_ref, sc_ref, ga_ref, wgu_ref, wdn_ref, xo_ref, acc_ref):
    x = x_ref[...]
    hb = _rms_modulate(x, g_ref[...], sh_ref[...], sc_ref[...]).astype(MXU_DTYPE)
    d_ff = wdn_ref.shape[0]
    for c in range(d_ff // FFN_CHUNK):
        cols = slice(c * FFN_CHUNK, (c + 1) * FFN_CHUNK)
        gate = _mm(hb, wgu_ref[:, cols])
        up = _mm(hb, wgu_ref[:, d_ff + c * FFN_CHUNK:d_ff + (c + 1) * FFN_CHUNK])
        act = (gate * jax.nn.sigmoid(gate) * up).astype(MXU_DTYPE)
        part = _mm(act, wdn_ref[cols, :])
        if c == 0:
            acc_ref[...] = part
        else:
            acc_ref[...] += part
    xo_ref[...] = x + ga_ref[...] * acc_ref[...]


def _ffn_call(x, g, shift, scale, ga2, w_gu, w_dn, tm):
    b, t, d = x.shape
    return pl.pallas_call(
        _ffn_kernel,
        out_shape=jax.ShapeDtypeStruct(x.shape, F32),
        grid=(b, t // tm),
        in_specs=[_row_spec(tm, d), _full_spec(g), _mod_spec(shift, tm, d), _mod_spec(scale, tm, d),
                  _mod_spec(ga2, tm, d), _full_spec(w_gu), _full_spec(w_dn)],
        out_specs=_row_spec(tm, d),
        scratch_shapes=[pltpu.VMEM((tm, d), F32)],
        compiler_params=_params("arbitrary", "arbitrary"),
        name="swiglu",
    )(x, g, shift, scale, ga2, w_gu, w_dn)


def _final_norm_kernel(x_ref, g_ref, o_ref):
    x = x_ref[...]
    o_ref[...] = x * lax.rsqrt(jnp.mean(x * x, axis=-1, keepdims=True) + RMS_EPS) * g_ref[...]


def _final_norm_call(x, g, tm):
    b, t, d = x.shape
    return pl.pallas_call(
        _final_norm_kernel,
        out_shape=jax.ShapeDtypeStruct(x.shape, F32),
        grid=(b, t // tm),
        in_specs=[_row_spec(tm, d), _full_spec(g)],
        out_specs=_row_spec(tm, d),
        compiler_params=_params("arbitrary", "arbitrary"),
        name="final_norm",
    )(x, g)


FFN_CHUNK = 256


def _row_tile(t):
    for tm in (512, 256, 128, 64, 32, 16, 8):
        if t % tm == 0:
            return tm
    return t


def kernel(x_prompt, x_sample, c_prompt, c_sample, cache_sb_kv, cache_moba_kv, page_table, w_ada, b_ada, g_mix,
           w_in, w_proj_sb, w_proj_moba, w_out, g_ffn, w_gate_up, w_down, g_final):
    n_b, t, d = x_prompt.shape
    n_s = x_sample.shape[0]
    depth, n_pool, page = cache_sb_kv.shape[:3]
    d_ff = w_down.shape[1]
    assert x_sample.shape[1] == 1 and d_ff % FFN_CHUNK == 0
    w = WIDTH

    w_in_b = w_in.astype(MXU_DTYPE)
    w_q = jnp.concatenate([w_in_b[:, :, 0:w], w_in_b[:, :, 3 * w:4 * w]], axis=2)
    w_kvt = jnp.concatenate([w_in_b[:, :, w:3 * w], w_in_b[:, :, 4 * w:6 * w]], axis=2).transpose(0, 2, 1)
    w_g = w_in_b[:, :, 6 * w:]
    w_ps_b = w_proj_sb.astype(MXU_DTYPE)
    w_pm_b = w_proj_moba.astype(MXU_DTYPE)
    w_o_b = w_out.astype(MXU_DTYPE)
    w_gu_b = w_gate_up.astype(MXU_DTYPE)
    w_dn_b = w_down.astype(MXU_DTYPE)

    to_pages = lambda cache: cache.transpose(0, 1, 3, 4, 5, 2).reshape(depth, n_pool, 2 * w, page)
    cache_sb_t, cache_mb_t = to_pages(cache_sb_kv), to_pages(cache_moba_kv)

    mod = _ada_call(jnp.concatenate([c_prompt, c_sample], axis=0), w_ada, b_ada)

    xp = x_prompt
    xs = x_sample.reshape(1, n_s, d)
    tm_p, tm_s = _row_tile(t), _row_tile(n_s)
    kv_p = [jnp.zeros((depth, n_b, 2 * w, t), F32)] * 2
    kv_s = [jnp.zeros((depth, 1, 2 * w, n_s), F32)] * 2
    for l in range(depth):
        mod_p = [mod[l, :n_b, k * d:(k + 1) * d].reshape(n_b, 1, d) for k in range(6)]
        mod_s = [mod[l, n_b:, k * d:(k + 1) * d].reshape(1, n_s, d) for k in range(6)]
        g_mix_l, g_ffn_l = g_mix[l].reshape(1, d), g_ffn[l].reshape(1, d)

        qs, qm, *kv_p, gates = _inproj_call(xp, g_mix_l, mod_p[0], mod_p[1], w_q[l], w_kvt[l], w_g[l], tm_p,
                                            l, depth, kv_p)
        o_sb = _sb_prompt_call(qs, kv_p[0], l)
        o_mb = _moba_prompt_call(qm, kv_p[1], l)
        xp = _merge_call(o_sb, o_mb, gates, xp, mod_p[2], w_ps_b[l], w_pm_b[l], w_o_b[l], tm_p)
        xp = _ffn_call(xp, g_ffn_l, mod_p[3], mod_p[4], mod_p[5], w_gu_b[l], w_dn_b[l], tm_p)

        qs, qm, *kv_s, gates = _inproj_call(xs, g_mix_l, mod_s[0], mod_s[1], w_q[l], w_kvt[l], w_g[l], tm_s,
                                            l, depth, kv_s)
        o_sb, o_mb = _decode_call(l, page_table, qs.reshape(n_s, 1, w), qm.reshape(n_s, 1, w), kv_s[1][l, 0],
                                  cache_sb_t, cache_mb_t)
        xs = _merge_call(o_sb.reshape(1, n_s, w), o_mb.reshape(1, n_s, w), gates, xs, mod_s[2],
                         w_ps_b[l], w_pm_b[l], w_o_b[l], tm_s)
        xs = _ffn_call(xs, g_ffn_l, mod_s[3], mod_s[4], mod_s[5], w_gu_b[l], w_dn_b[l], tm_s)

    g_fin = g_final.reshape(1, d)
    y_prompt = _final_norm_call(xp, g_fin, tm_p)
    y_sample = _final_norm_call(xs, g_fin, tm_s).reshape(n_s, 1, d)

    def kv_prompt(kvt):
        return kvt.reshape(depth, n_b, 2, N_HEADS, HEAD_DIM, t).transpose(0, 1, 5, 2, 3, 4)

    def kv_sample(kvt):
        return kvt.reshape(depth, 1, 2, N_HEADS, HEAD_DIM, n_s).transpose(0, 5, 1, 2, 3, 4)

    return (y_prompt, y_sample, kv_prompt(kv_p[0]), kv_prompt(kv_p[1]), kv_sample(kv_s[0]), kv_sample(kv_s[1]))
```

```python
import functools

import jax
import jax.numpy as jnp
from jax import lax
from jax.experimental import pallas as pl
from jax.experimental.pallas import tpu as pltpu

HEAD_DIM = 64
N_HEADS = 8
WIDTH = N_HEADS * HEAD_DIM
MOBA_BLOCK = 256
MOBA_TOPK = 3
RMS_EPS = 1e-6
ATT_SCALE = HEAD_DIM ** -0.5
ALIBI_SLOPES = tuple(2.0 ** (-8.0 * (h + 1) / N_HEADS) for h in range(N_HEADS))
NEG = -1e30
SB_DEAD = -104.0
SB_TAIL_KEYS = 256
SB_TILE = 256
MXU_DTYPE = jnp.bfloat16
F32 = jnp.float32
VMEM_LIMIT_BYTES = 56 * 1024 * 1024
NT_DIMS = (((1,), (1,)), ((), ()))


def _params(*semantics):
    return pltpu.CompilerParams(dimension_semantics=semantics, vmem_limit_bytes=VMEM_LIMIT_BYTES)


def _mm(a, b):
    return jnp.dot(a, b, preferred_element_type=F32)


def _mm_nt(a, b):
    return lax.dot_general(a, b, NT_DIMS, preferred_element_type=F32)


def _neg_softplus(z):
    return -(jnp.maximum(z, 0.0) + jnp.log(1.0 + jnp.exp(-jnp.abs(z))))


def _split_hi_lo(x):
    hi = x.astype(MXU_DTYPE)
    lo = (x - hi.astype(F32)).astype(MXU_DTYPE)
    return hi, lo


def _rms_modulate(x, g, shift, scale):
    y = x * lax.rsqrt(jnp.mean(x * x, axis=-1, keepdims=True) + RMS_EPS)
    return (y * g) * (1.0 + scale) + shift


def _ada_kernel(c_ref, w_ref, b_ref, o_ref):
    c = c_ref[...]
    a = (c * jax.nn.sigmoid(c)).astype(MXU_DTYPE)
    o_ref[...] = _mm(a, w_ref[...].astype(MXU_DTYPE)) + b_ref[...]


def _ada_call(c_all, w_ada, b_ada, tn=1536):
    depth, d, n = w_ada.shape
    m = c_all.shape[0]
    return pl.pallas_call(
        _ada_kernel,
        out_shape=jax.ShapeDtypeStruct((depth, m, n), F32),
        grid=(depth, n // tn),
        in_specs=[
            pl.BlockSpec((m, d), lambda l, j: (0, 0)),
            pl.BlockSpec((None, d, tn), lambda l, j: (l, 0, j)),
            pl.BlockSpec((None, 1, tn), lambda l, j: (l, 0, j)),
        ],
        out_specs=pl.BlockSpec((None, m, tn), lambda l, j: (l, 0, j)),
        compiler_params=_params("arbitrary", "arbitrary"),
        name="adaln",
    )(c_all, w_ada, b_ada.reshape(depth, 1, n))


def _inproj_kernel(x_ref, g_ref, sh_ref, sc_ref, wq_ref, wkvt_ref, wg_ref, *refs):
    qs_ref, qm_ref, kvts_ref, kvtm_ref, gate_ref = refs[-5:]
    hb = _rms_modulate(x_ref[...], g_ref[...], sh_ref[...], sc_ref[...]).astype(MXU_DTYPE)
    w = WIDTH
    qs_ref[...] = (_mm(hb, wq_ref[:, :w]) * ATT_SCALE).astype(qs_ref.dtype)
    qm_ref[...] = (_mm(hb, wq_ref[:, w:]) * ATT_SCALE).astype(qm_ref.dtype)
    kvts_ref[...] = _mm_nt(wkvt_ref[:2 * w, :], hb)
    kvtm_ref[...] = _mm_nt(wkvt_ref[2 * w:, :], hb)
    gate_ref[...] = _mm(hb, wg_ref[...])


def _row_spec(tm, d):
    return pl.BlockSpec((None, tm, d), lambda b, i: (b, i, 0))


def _mod_spec(mod, tm, d):
    if mod.shape[1] == 1:
        return pl.BlockSpec((None, 1, d), lambda b, i: (b, 0, 0))
    return pl.BlockSpec((None, tm, d), lambda b, i: (b, i, 0))


def _full_spec(a):
    nd = a.ndim
    return pl.BlockSpec(a.shape, lambda b, i: (0,) * nd)


def _inproj_call(x, g, shift, scale, w_q, w_kvt, w_g, tm, layer, depth, kv_stacks):
    b, t, d = x.shape
    row = lambda width, dtype: jax.ShapeDtypeStruct((b, t, width), dtype)
    kvt = jax.ShapeDtypeStruct((depth, b, 2 * WIDTH, t), F32)
    kvt_spec = pl.BlockSpec((None, None, 2 * WIDTH, tm), lambda bb, i: (layer, bb, 0, i))
    operands = [x, g, shift, scale, w_q, w_kvt, w_g]
    in_specs = [_row_spec(tm, d), _full_spec(g), _mod_spec(shift, tm, d), _mod_spec(scale, tm, d),
                _full_spec(w_q), _full_spec(w_kvt), _full_spec(w_g)]
    aliases = {len(operands): 2, len(operands) + 1: 3}
    operands += list(kv_stacks)
    in_specs += [pl.BlockSpec(memory_space=pl.ANY)] * 2
    return pl.pallas_call(
        _inproj_kernel,
        out_shape=(row(WIDTH, MXU_DTYPE), row(WIDTH, MXU_DTYPE), kvt, kvt, row(w_g.shape[-1], F32)),
        grid=(b, t // tm),
        in_specs=in_specs,
        out_specs=(_row_spec(tm, WIDTH), _row_spec(tm, WIDTH), kvt_spec, kvt_spec, _row_spec(tm, w_g.shape[-1])),
        input_output_aliases=aliases,
        compiler_params=_params("arbitrary", "arbitrary"),
        name="inproj",
    )(*operands)


LANES = 128
HEAD_PAIRS = [slice((h // 2) * LANES, (h // 2 + 1) * LANES) for h in range(N_HEADS)]


def _own_lanes(h, rows):
    lane = lax.broadcasted_iota(jnp.int32, (rows, LANES), 1)
    return lane < HEAD_DIM if h % 2 == 0 else lane >= HEAD_DIM


def _spare_lane(h, rows, k):
    lane = lax.broadcasted_iota(jnp.int32, (rows, LANES), 1)
    return lane == (HEAD_DIM + k if h % 2 == 0 else k)


def _stage_kv(kvt_ref, ka_ref, vt_ref, tile, km_ref=None, key_extras=None):
    for c in range(ka_ref.shape[0]):
        k_rows = kvt_ref[:WIDTH, c * tile:(c + 1) * tile].T
        vt_ref[c] = kvt_ref[WIDTH:, c * tile:(c + 1) * tile].astype(MXU_DTYPE)
        if key_extras is None:
            ka_ref[c] = k_rows.astype(MXU_DTYPE)
            continue
        for h in range(N_HEADS):
            ka_ref[c, h] = jnp.where(_own_lanes(h, tile), k_rows[:, HEAD_PAIRS[h]],
                                     key_extras(h)).astype(MXU_DTYPE)
        if km_ref is not None:
            km_ref[c:c + 1, :] = (jnp.sum(k_rows, axis=0, keepdims=True) * (1.0 / tile)).astype(MXU_DTYPE)


def _sb_prompt_kernel(q_ref, kv_ref, o_ref, kb_ref, vt_ref, carry_ref, ot_ref, *, tile):
    i = pl.program_id(1)

    @pl.when(i == 0)
    def _():
        _stage_kv(kv_ref, kb_ref, vt_ref, tile)

    row = lax.broadcasted_iota(jnp.int32, (tile, tile), 0)
    col = lax.broadcasted_iota(jnp.int32, (tile, tile), 1)
    before = row < col
    later = jnp.where(before, 1.0, 0.0).astype(MXU_DTYPE)

    heads = [slice(h * HEAD_DIM, (h + 1) * HEAD_DIM) for h in range(N_HEADS)]

    def key_tile(j, diagonal):
        zs = [_mm_nt(kb_ref[j, :, hs], q_ref[:, hs]) for hs in heads]
        lfs, sums = [], []
        for z in zs:
            lf = _neg_softplus(z)
            lfs.append(lf)
            hi, lo = _split_hi_lo(jnp.where(before, lf, 0.0) if diagonal else lf)
            sums.append(_mm(later, hi) + _mm(later, lo))
        for h, (hs, z, lf, sm) in enumerate(zip(heads, zs, lfs, sums)):
            total = jnp.sum(jnp.where(before, lf, 0.0) if diagonal else lf, axis=0, keepdims=True)
            if diagonal:
                a = jnp.where(before, jnp.exp(z + lf + sm), 0.0)
                ot_ref[hs, :] = _mm(vt_ref[j, hs, :], a.astype(MXU_DTYPE))
                carry_ref[h:h + 1, :] = total
            else:
                carry = carry_ref[h:h + 1, :]
                a = jnp.exp(z + lf + (sm + carry))
                ot_ref[hs, :] += _mm(vt_ref[j, hs, :], a.astype(MXU_DTYPE))
                carry_ref[h:h + 1, :] = carry + total

    def alive():
        return (jnp.max(carry_ref[...]) > SB_DEAD).astype(jnp.int32)

    def walk(state):
        j, _ = state
        key_tile(j, False)
        return j - 1, alive()

    key_tile(i, True)
    lax.while_loop(lambda state: jnp.logical_and(state[0] >= 0, state[1] > 0), walk, (i - 1, alive()))
    o_ref[...] = ot_ref[...].T.astype(o_ref.dtype)


def _moba_prompt_kernel(q_ref, kv_ref, o_ref, ka_ref, vt_ref, km_ref, nb_ref, qa_ref, m_ref, l_ref, ot_ref, *,
                        tile):
    i = pl.program_id(1)
    n_blk = ka_ref.shape[0]
    assert tile <= 256
    tile_row = lax.broadcasted_iota(jnp.int32, (tile, LANES), 0).astype(F32)

    def key_extras(h):
        return jnp.where(_spare_lane(h, tile, 0), tile_row, jnp.where(_spare_lane(h, tile, 1), 1.0, 0.0))

    @pl.when(i == 0)
    def _():
        _stage_kv(kv_ref, ka_ref, vt_ref, tile, km_ref, key_extras)

    for h in range(N_HEADS):
        slope = ALIBI_SLOPES[h]
        extras = jnp.where(_spare_lane(h, tile, 0), slope,
                           jnp.where(_spare_lane(h, tile, 1), -slope * tile_row, 0.0))
        qa_ref[h] = jnp.where(_own_lanes(h, tile), q_ref[:, HEAD_PAIRS[h]].astype(F32), extras).astype(MXU_DTYPE)

    row = lax.broadcasted_iota(jnp.int32, (tile, tile), 0)
    col = lax.broadcasted_iota(jnp.int32, (tile, tile), 1)
    causal = row <= col
    blk = lax.broadcasted_iota(jnp.int32, (n_blk, tile), 0)
    heads = [slice(h * HEAD_DIM, (h + 1) * HEAD_DIM) for h in range(N_HEADS)]
    lanes = [slice(h * tile, (h + 1) * tile) for h in range(N_HEADS)]
    slope_lanes = jnp.concatenate([jnp.full((1, tile), s, F32) for s in ALIBI_SLOPES], axis=1)

    def scores(j, fix=lambda s: s):
        return jnp.concatenate([fix(_mm_nt(ka_ref[j, h], qa_ref[h])) for h in range(N_HEADS)], axis=1)

    for h, ls in enumerate(lanes):
        km_h = jnp.where(_own_lanes(h, n_blk), km_ref[:, HEAD_PAIRS[h]], jnp.zeros((), MXU_DTYPE))
        gate = _mm_nt(km_h, q_ref[:, HEAD_PAIRS[h]])
        rank = jnp.zeros((n_blk, tile), F32)
        for n2 in range(n_blk):
            g2 = gate[n2:n2 + 1, :]
            beats = (g2 > gate) | ((g2 == gate) & (n2 < blk))
            rank = rank + jnp.where(beats, 1.0, 0.0) * jnp.where(n2 < i, 1.0, 0.0)
        chosen = (blk < i) & (rank < MOBA_TOPK)
        nb_ref[:, ls] = jnp.where(chosen, 0.0, NEG)

    s = scores(i, lambda s: jnp.where(causal, s, NEG))
    m = jnp.max(s, axis=0, keepdims=True)
    p = jnp.exp(s - m)
    m_ref[...] = m
    l_ref[...] = jnp.sum(p, axis=0, keepdims=True)
    pb = p.astype(MXU_DTYPE)
    for hs, ls in zip(heads, lanes):
        ot_ref[hs, :] = _mm(vt_ref[i, hs, :], pb[:, ls])

    def past_block(j, c):
        s = scores(j) + (nb_ref[pl.ds(j, 1), :] + slope_lanes * (tile * (j - i)).astype(F32))
        m = m_ref[...]
        m_new = jnp.maximum(m, jnp.max(s, axis=0, keepdims=True))
        corr = jnp.exp(m - m_new)
        p = jnp.exp(s - m_new)
        l_ref[...] = l_ref[...] * corr + jnp.sum(p, axis=0, keepdims=True)
        m_ref[...] = m_new
        pb = p.astype(MXU_DTYPE)
        for hs, ls in zip(heads, lanes):
            ot_ref[hs, :] = ot_ref[hs, :] * corr[:, ls] + _mm(vt_ref[j, hs, :], pb[:, ls])
        return c

    lax.fori_loop(0, i, past_block, 0)
    l = l_ref[...]
    for hs, ls in zip(heads, lanes):
        ot_ref[hs, :] = ot_ref[hs, :] / l[:, ls]
    o_ref[...] = ot_ref[...].T.astype(o_ref.dtype)


def _prompt_attention_call(kernel_fn, q, kv_stack, layer, tile, extra_scratch, name):
    b, t, _ = q.shape
    assert t % tile == 0
    n_t = t // tile
    scratch = extra_scratch(n_t, tile)
    scratch = scratch[:1] + [pltpu.VMEM((n_t, WIDTH, tile), MXU_DTYPE)] + scratch[1:] + [pltpu.VMEM((WIDTH, tile), F32)]
    return pl.pallas_call(
        functools.partial(kernel_fn, tile=tile),
        out_shape=jax.ShapeDtypeStruct((b, t, WIDTH), MXU_DTYPE),
        grid=(b, n_t),
        in_specs=[pl.BlockSpec((None, tile, WIDTH), lambda bb, i: (bb, i, 0)),
                  pl.BlockSpec((None, None, 2 * WIDTH, t), lambda bb, i: (layer, bb, 0, 0))],
        out_specs=pl.BlockSpec((None, tile, WIDTH), lambda bb, i: (bb, i, 0)),
        scratch_shapes=scratch,
        compiler_params=_params("arbitrary", "arbitrary"),
        name=name,
    )(q, kv_stack)


def _sb_prompt_call(q, kv_stack, layer):
    extra = lambda n_t, tile: [pltpu.VMEM((n_t, tile, WIDTH), MXU_DTYPE), pltpu.VMEM((N_HEADS, tile), F32)]
    return _prompt_attention_call(_sb_prompt_kernel, q, kv_stack, layer, SB_TILE, extra, "sb_prompt")


def _moba_prompt_call(q, kv_stack, layer):
    extra = lambda n_t, tile: [pltpu.VMEM((n_t, N_HEADS, tile, LANES), MXU_DTYPE),
                               pltpu.VMEM((n_t, WIDTH), MXU_DTYPE), pltpu.VMEM((n_t, N_HEADS * tile), F32),
                               pltpu.VMEM((N_HEADS, tile, LANES), MXU_DTYPE),
                               pltpu.VMEM((1, N_HEADS * tile), F32), pltpu.VMEM((1, N_HEADS * tile), F32)]
    return _prompt_attention_call(_moba_prompt_kernel, q, kv_stack, layer, MOBA_BLOCK, extra, "moba_prompt")


def _decode_kernel(pt_ref, layer_ref, qs_ref, qm_ref, kvtn_ref, sb_hbm_ref, mb_hbm_ref, osb_ref, omb_ref,
                   km_ref, kvn_ref, acc_ref, carry_ref, page_ref, page_sem, pages_ref, pages_sem, *,
                   n_seq, n_pages, n_tail, page):
    b = pl.program_id(0)
    slot = b % 2

    def page_copies(seq, into):
        first = seq * n_pages
        sources = [sb_hbm_ref.at[layer_ref[0], pt_ref[first + n_pages - n_tail + k]] for k in range(n_tail)]
        sources += [mb_hbm_ref.at[layer_ref[0], pt_ref[first + p]] for p in range(n_pages)]
        return [pltpu.make_async_copy(src, pages_ref.at[into, k], pages_sem.at[into])
                for k, src in enumerate(sources)]

    @pl.when(b == 0)
    def _():
        for copy in page_copies(0, 0):
            copy.start()

    @pl.when(b + 1 < n_seq)
    def _():
        for copy in page_copies(b + 1, 1 - slot):
            copy.start()

    for copy in page_copies(b, slot):
        copy.wait()
    sb_tail = [pages_ref.at[slot, k] for k in range(n_tail)]
    mb_pages = [pages_ref.at[slot, n_tail + p] for p in range(n_pages)]
    past = n_pages * page
    pages_per_blk = MOBA_BLOCK // page
    n_blk = past // MOBA_BLOCK

    @pl.when(b == 0)
    def _():
        kvn_ref[...] = kvtn_ref[...].T

    head_of_lane = lax.broadcasted_iota(jnp.int32, (N_HEADS, WIDTH), 1) // HEAD_DIM
    head_row = lax.broadcasted_iota(jnp.int32, (N_HEADS, WIDTH), 0)
    own_cols = head_of_lane == head_row

    def spread(q_row):
        return jnp.where(own_cols, q_row.astype(F32), 0.0)

    def gather_heads(x):
        return jnp.sum(jnp.where(own_cols, x, 0.0), axis=0, keepdims=True)

    qb = spread(qs_ref[...]).astype(MXU_DTYPE)
    r = lax.broadcasted_iota(jnp.int32, (page, 2 * page), 0)
    c = lax.broadcasted_iota(jnp.int32, (page, 2 * page), 1)
    later_and_total = jnp.where((r > c) | (c >= page), 1.0, 0.0).astype(MXU_DTYPE)

    def sb_pages_terms(page_refs, carry):
        zs = [_mm(qb, ref[:WIDTH, :].astype(MXU_DTYPE)) for ref in page_refs]
        lfs = [_neg_softplus(z) for z in zs]
        sums = []
        for lf in lfs:
            hi, lo = _split_hi_lo(lf)
            sums.append(_mm(hi, later_and_total) + _mm(lo, later_and_total))
        acc = jnp.zeros((N_HEADS, WIDTH), F32)
        for ref, z, lf, sm in zip(page_refs, zs, lfs, sums):
            a = jnp.exp(z + lf + (carry + sm[:, :page]))
            acc = acc + _mm_nt(a.astype(MXU_DTYPE), ref[WIDTH:, :].astype(MXU_DTYPE))
            carry = carry + sm[:, page:]
        return acc, carry

    acc, carry = sb_pages_terms(sb_tail[::-1], jnp.zeros((N_HEADS, page), F32))
    acc_ref[...] = acc
    carry_ref[...] = carry

    qb_m = spread(qm_ref[...]).astype(MXU_DTYPE)
    for n in range(n_blk):
        tot = mb_pages[n * pages_per_blk][:WIDTH, :]
        for p in range(n * pages_per_blk + 1, (n + 1) * pages_per_blk):
            tot = tot + mb_pages[p][:WIDTH, :]
        km_ref[:, n:n + 1] = jnp.sum(tot, axis=1, keepdims=True) * (1.0 / MOBA_BLOCK)
    gate = _mm(qb_m, km_ref[...].astype(MXU_DTYPE))
    blk = lax.broadcasted_iota(jnp.int32, (N_HEADS, n_blk), 1)
    rank = jnp.zeros((N_HEADS, n_blk), F32)
    for n2 in range(n_blk):
        g2 = gate[:, n2:n2 + 1]
        beats = (g2 > gate) | ((g2 == gate) & (n2 < blk))
        rank = rank + jnp.where(beats, 1.0, 0.0)
    not_chosen = jnp.where(rank < MOBA_TOPK, 0.0, NEG)

    slope = jnp.zeros((N_HEADS, page), F32)
    hrow = lax.broadcasted_iota(jnp.int32, (N_HEADS, page), 0)
    for h in range(N_HEADS):
        slope = jnp.where(hrow == h, ALIBI_SLOPES[h], slope)
    lane = lax.broadcasted_iota(jnp.int32, (N_HEADS, page), 1)

    kv_new = kvn_ref[pl.ds(b, 1), :].astype(MXU_DTYPE).astype(F32)
    s_own = jnp.sum(qb_m.astype(F32) * kv_new[:, :WIDTH], axis=1, keepdims=True)
    scores = []
    m = s_own
    for p in range(n_pages):
        dist = (past - p * page - lane).astype(F32)
        n = p // pages_per_blk
        s = _mm(qb_m, mb_pages[p][:WIDTH, :].astype(MXU_DTYPE)) - slope * dist + not_chosen[:, n:n + 1]
        scores.append(s)
        m = jnp.maximum(m, jnp.max(s, axis=1, keepdims=True))
    p_own = jnp.exp(s_own - m)
    l = p_own
    acc_m = p_own.astype(MXU_DTYPE).astype(F32) * kv_new[:, WIDTH:]
    for p in range(n_pages):
        w = jnp.exp(scores[p] - m)
        l = l + jnp.sum(w, axis=1, keepdims=True)
        acc_m = acc_m + _mm_nt(w.astype(MXU_DTYPE), mb_pages[p][WIDTH:, :].astype(MXU_DTYPE))
    omb_ref[...] = gather_heads(acc_m / l).astype(omb_ref.dtype)

    def alive():
        return (jnp.max(carry_ref[...]) > SB_DEAD).astype(jnp.int32)

    def older_page(state):
        p, _ = state
        fetch = pltpu.make_async_copy(sb_hbm_ref.at[layer_ref[0], pt_ref[b * n_pages + p]], page_ref, page_sem)
        fetch.start()
        fetch.wait()
        acc, carry = sb_pages_terms([page_ref], carry_ref[...])
        acc_ref[...] += acc
        carry_ref[...] = carry
        return p - 1, alive()

    lax.while_loop(lambda state: jnp.logical_and(state[0] >= 0, state[1] > 0), older_page,
                   (n_pages - n_tail - 1, alive()))
    osb_ref[...] = gather_heads(acc_ref[...]).astype(osb_ref.dtype)


def _decode_call(layer, page_table, q_sb, q_mb, kvt_mb_new, cache_sb, cache_mb):
    n_seq, n_pages = page_table.shape
    page = cache_sb.shape[-1]
    assert (n_pages * page) % MOBA_BLOCK == 0 and MOBA_BLOCK % page == 0
    n_tail = min(n_pages, SB_TAIL_KEYS // page)

    def seq_spec(width):
        return pl.BlockSpec((None, 1, width), lambda b, pt, lay: (b, 0, 0))

    grid_spec = pltpu.PrefetchScalarGridSpec(
        num_scalar_prefetch=2,
        grid=(n_seq,),
        in_specs=[seq_spec(WIDTH), seq_spec(WIDTH), pl.BlockSpec((2 * WIDTH, n_seq), lambda b, pt, lay: (0, 0)),
                  pl.BlockSpec(memory_space=pl.ANY), pl.BlockSpec(memory_space=pl.ANY)],
        out_specs=(seq_spec(WIDTH), seq_spec(WIDTH)),
        scratch_shapes=[pltpu.VMEM((WIDTH, n_pages * page // MOBA_BLOCK), F32),
                        pltpu.VMEM((n_seq, 2 * WIDTH), F32),
                        pltpu.VMEM((N_HEADS, WIDTH), F32),
                        pltpu.VMEM((N_HEADS, page), F32),
                        pltpu.VMEM((2 * WIDTH, page), F32),
                        pltpu.SemaphoreType.DMA(()),
                        pltpu.VMEM((2, n_tail + n_pages, 2 * WIDTH, page), F32),
                        pltpu.SemaphoreType.DMA((2,))],
    )
    out_shape = (jax.ShapeDtypeStruct((n_seq, 1, WIDTH), MXU_DTYPE),) * 2
    return pl.pallas_call(
        functools.partial(_decode_kernel, n_seq=n_seq, n_pages=n_pages, n_tail=n_tail, page=page),
        out_shape=out_shape,
        grid_spec=grid_spec,
        compiler_params=_params("arbitrary"),
        name="decode_attention",
    )(page_table.reshape(-1), jnp.full((1,), layer, jnp.int32), q_sb, q_mb, kvt_mb_new, cache_sb, cache_mb)


def _merge_kernel(osb_ref, omb_ref, gate_ref, x_ref, ga_ref, wps_ref, wpm_ref, wo_ref, xo_ref):
    d = x_ref.shape[-1]
    merged = (jax.nn.sigmoid(gate_ref[:, :d]) * _mm(osb_ref[...], wps_ref[...])
              + jax.nn.sigmoid(gate_ref[:, d:]) * _mm(omb_ref[...], wpm_ref[...]))
    xo_ref[...] = x_ref[...] + ga_ref[...] * _mm(merged.astype(MXU_DTYPE), wo_ref[...])


def _merge_call(o_sb, o_mb, gates, x, ga1, w_ps, w_pm, w_o, tm):
    b, t, d = x.shape
    return pl.pallas_call(
        _merge_kernel,
        out_shape=jax.ShapeDtypeStruct(x.shape, F32),
        grid=(b, t // tm),
        in_specs=[_row_spec(tm, WIDTH), _row_spec(tm, WIDTH), _row_spec(tm, gates.shape[-1]), _row_spec(tm, d),
                  _mod_spec(ga1, tm, d), _full_spec(w_ps), _full_spec(w_pm), _full_spec(w_o)],
        out_specs=_row_spec(tm, d),
        compiler_params=_params("arbitrary", "arbitrary"),
        name="merge_outproj",
    )(o_sb, o_mb, gates, x, ga1, w_ps, w_pm, w_o)


def _ffn_kernel(x_ref, g_ref, sh_ref, sc_ref, ga_ref, wgu_ref, wdn_ref, xo_ref, acc_ref):
    x = x_ref[...]
    hb = _rms_modulate(x, g_ref[...], sh_ref[...], sc_ref[...]).astype(MXU_DTYPE)
    d_ff = wdn_ref.shape[0]
    for c in range(d_ff // FFN_CHUNK):
        cols = slice(c * FFN_CHUNK, (c + 1) * FFN_CHUNK)
        gate = _mm(hb, wgu_ref[:, cols])
        up = _mm(hb, wgu_ref[:, d_ff + c * FFN_CHUNK:d_ff + (c + 1) * FFN_CHUNK])
        act = (gate * jax.nn.sigmoid(gate) * up).astype(MXU_DTYPE)
        part = _mm(act, wdn_ref[cols, :])
        if c == 0:
            acc_ref[...] = part
        else:
            acc_ref[...] += part
    xo_ref[...] = x + ga_ref[...] * acc_ref[...]


def _ffn_call(x, g, shift, scale, ga2, w_gu, w_dn, tm):
    b, t, d = x.shape
    return pl.pallas_call(
        _ffn_kernel,
        out_shape=jax.ShapeDtypeStruct(x.shape, F32),
        grid=(b, t // tm),
        in_specs=[_row_spec(tm, d), _full_spec(g), _mod_spec(shift, tm, d), _mod_spec(scale, tm, d),
                  _mod_spec(ga2, tm, d), _full_spec(w_gu), _full_spec(w_dn)],
        out_specs=_row_spec(tm, d),
        scratch_shapes=[pltpu.VMEM((tm, d), F32)],
        compiler_params=_params("arbitrary", "arbitrary"),
        name="swiglu",
    )(x, g, shift, scale, ga2, w_gu, w_dn)


def _final_norm_kernel(x_ref, g_ref, o_ref):
    x = x_ref[...]
    o_ref[...] = x * lax.rsqrt(jnp.mean(x * x, axis=-1, keepdims=True) + RMS_EPS) * g_ref[...]


def _final_norm_call(x, g, tm):
    b, t, d = x.shape
    return pl.pallas_call(
        _final_norm_kernel,
        out_shape=jax.ShapeDtypeStruct(x.shape, F32),
        grid=(b, t // tm),
        in_specs=[_row_spec(tm, d), _full_spec(g)],
        out_specs=_row_spec(tm, d),
        compiler_params=_params("arbitrary", "arbitrary"),
        name="final_norm",
    )(x, g)


FFN_CHUNK = 256


def _row_tile(t):
    for tm in (512, 256, 128, 64, 32, 16, 8):
        if t % tm == 0:
            return tm
    return t


def kernel(x_prompt, x_sample, c_prompt, c_sample, cache_sb_kv, cache_moba_kv, page_table, w_ada, b_ada, g_mix,
           w_in, w_proj_sb, w_proj_moba, w_out, g_ffn, w_gate_up, w_down, g_final):
    n_b, t, d = x_prompt.shape
    n_s = x_sample.shape[0]
    depth, n_pool, page = cache_sb_kv.shape[:3]
    d_ff = w_down.shape[1]
    assert x_sample.shape[1] == 1 and d_ff % FFN_CHUNK == 0
    w = WIDTH

    w_in_b = w_in.astype(MXU_DTYPE)
    w_q = jnp.concatenate([w_in_b[:, :, 0:w], w_in_b[:, :, 3 * w:4 * w]], axis=2)
    w_kvt = jnp.concatenate([w_in_b[:, :, w:3 * w], w_in_b[:, :, 4 * w:6 * w]], axis=2).transpose(0, 2, 1)
    w_g = w_in_b[:, :, 6 * w:]
    w_ps_b = w_proj_sb.astype(MXU_DTYPE)
    w_pm_b = w_proj_moba.astype(MXU_DTYPE)
    w_o_b = w_out.astype(MXU_DTYPE)
    w_gu_b = w_gate_up.astype(MXU_DTYPE)
    w_dn_b = w_down.astype(MXU_DTYPE)

    to_pages = lambda cache: cache.transpose(0, 1, 3, 4, 5, 2).reshape(depth, n_pool, 2 * w, page)
    cache_sb_t, cache_mb_t = to_pages(cache_sb_kv), to_pages(cache_moba_kv)

    mod = _ada_call(jnp.concatenate([c_prompt, c_sample], axis=0), w_ada, b_ada)

    xp = x_prompt
    xs = x_sample.reshape(1, n_s, d)
    tm_p, tm_s = _row_tile(t), _row_tile(n_s)
    kv_p = [jnp.zeros((depth, n_b, 2 * w, t), F32)] * 2
    kv_s = [jnp.zeros((depth, 1, 2 * w, n_s), F32)] * 2
    for l in range(depth):
        mod_p = [mod[l, :n_b, k * d:(k + 1) * d].reshape(n_b, 1, d) for k in range(6)]
        mod_s = [mod[l, n_b:, k * d:(k + 1) * d].reshape(1, n_s, d) for k in range(6)]
        g_mix_l, g_ffn_l = g_mix[l].reshape(1, d), g_ffn[l].reshape(1, d)

        qs, qm, *kv_p, gates = _inproj_call(xp, g_mix_l, mod_p[0], mod_p[1], w_q[l], w_kvt[l], w_g[l], tm_p,
                                            l, depth, kv_p)
        o_sb = _sb_prompt_call(qs, kv_p[0], l)
        o_mb = _moba_prompt_call(qm, kv_p[1], l)
        xp = _merge_call(o_sb, o_mb, gates, xp, mod_p[2], w_ps_b[l], w_pm_b[l], w_o_b[l], tm_p)
        xp = _ffn_call(xp, g_ffn_l, mod_p[3], mod_p[4], mod_p[5], w_gu_b[l], w_dn_b[l], tm_p)

        qs, qm, *kv_s, gates = _inproj_call(xs, g_mix_l, mod_s[0], mod_s[1], w_q[l], w_kvt[l], w_g[l], tm_s,
                                            l, depth, kv_s)
        o_sb, o_mb = _decode_call(l, page_table, qs.reshape(n_s, 1, w), qm.reshape(n_s, 1, w), kv_s[1][l, 0],
                                  cache_sb_t, cache_mb_t)
        xs = _merge_call(o_sb.reshape(1, n_s, w), o_mb.reshape(1, n_s, w), gates, xs, mod_s[2],
                         w_ps_b[l], w_pm_b[l], w_o_b[l], tm_s)
        xs = _ffn_call(xs, g_ffn_l, mod_s[3], mod_s[4], mod_s[5], w_gu_b[l], w_dn_b[l], tm_s)

    g_fin = g_final.reshape(1, d)
    y_prompt = _final_norm_call(xp, g_fin, tm_p)
    y_sample = _final_norm_call(xs, g_fin, tm_s).reshape(n_s, 1, d)

    def kv_prompt(kvt):
        return kvt.reshape(depth, n_b, 2, N_HEADS, HEAD_DIM, t).transpose(0, 1, 5, 2, 3, 4)

    def kv_sample(kvt):
        return kvt.reshape(depth, 1, 2, N_HEADS, HEAD_DIM, n_s).transpose(0, 5, 1, 2, 3, 4)

    return (y_prompt, y_sample, kv_prompt(kv_p[0]), kv_prompt(kv_p[1]), kv_sample(kv_s[0]), kv_sample(kv_s[1]))
```

```python
import functools

import jax
import jax.numpy as jnp
from jax import lax
from jax.experimental import pallas as pl
from jax.experimental.pallas import tpu as pltpu

HEAD_DIM = 64
N_HEADS = 8
WIDTH = N_HEADS * HEAD_DIM
MOBA_BLOCK = 256
MOBA_TOPK = 3
RMS_EPS = 1e-6
ATT_SCALE = HEAD_DIM ** -0.5
ALIBI_SLOPES = tuple(2.0 ** (-8.0 * (h + 1) / N_HEADS) for h in range(N_HEADS))
NEG = -1e30
SB_DEAD = -104.0
SB_TAIL_KEYS = 256
MOBA_HEAD_GROUP = 8
SB_TILE = 256
MXU_DTYPE = jnp.bfloat16
F32 = jnp.float32
VMEM_LIMIT_BYTES = 56 * 1024 * 1024
NT_DIMS = (((1,), (1,)), ((), ()))


def _params(*semantics):
    return pltpu.CompilerParams(dimension_semantics=semantics, vmem_limit_bytes=VMEM_LIMIT_BYTES)


def _mm(a, b):
    return jnp.dot(a, b, preferred_element_type=F32)


def _mm_nt(a, b):
    return lax.dot_general(a, b, NT_DIMS, preferred_element_type=F32)


def _neg_softplus(z):
    nz = -z
    return jnp.minimum(nz, 0.0) - jnp.log(1.0 + jnp.exp(jnp.minimum(z, nz)))


def _split_hi_lo(x):
    hi = x.astype(MXU_DTYPE)
    lo = (x - hi.astype(F32)).astype(MXU_DTYPE)
    return hi, lo


def _rms_modulate(x, g, shift, scale):
    y = x * lax.rsqrt(jnp.mean(x * x, axis=-1, keepdims=True) + RMS_EPS)
    return (y * g) * (1.0 + scale) + shift


def _ada_kernel(c_ref, w_ref, b_ref, o_ref):
    c = c_ref[...]
    a = (c * jax.nn.sigmoid(c)).astype(MXU_DTYPE)
    o_ref[...] = _mm(a, w_ref[...].astype(MXU_DTYPE)) + b_ref[...]


def _ada_call(c_all, w_ada, b_ada, tn=1536):
    depth, d, n = w_ada.shape
    m = c_all.shape[0]
    return pl.pallas_call(
        _ada_kernel,
        out_shape=jax.ShapeDtypeStruct((depth, m, n), F32),
        grid=(depth, n // tn),
        in_specs=[
            pl.BlockSpec((m, d), lambda l, j: (0, 0)),
            pl.BlockSpec((None, d, tn), lambda l, j: (l, 0, j)),
            pl.BlockSpec((None, 1, tn), lambda l, j: (l, 0, j)),
        ],
        out_specs=pl.BlockSpec((None, m, tn), lambda l, j: (l, 0, j)),
        compiler_params=_params("arbitrary", "arbitrary"),
        name="adaln",
    )(c_all, w_ada, b_ada.reshape(depth, 1, n))


def _inproj_kernel(x_ref, g_ref, sh_ref, sc_ref, wq_ref, wkvt_ref, wg_ref, *refs):
    qs_ref, qm_ref, kvts_ref, kvtm_ref, gate_ref = refs[-5:]
    hb = _rms_modulate(x_ref[...], g_ref[...], sh_ref[...], sc_ref[...]).astype(MXU_DTYPE)
    w = WIDTH
    qs_ref[...] = (_mm(hb, wq_ref[:, :w]) * ATT_SCALE).astype(qs_ref.dtype)
    qm_ref[...] = (_mm(hb, wq_ref[:, w:]) * ATT_SCALE).astype(qm_ref.dtype)
    kvts_ref[...] = _mm_nt(wkvt_ref[:2 * w, :], hb)
    kvtm_ref[...] = _mm_nt(wkvt_ref[2 * w:, :], hb)
    gate_ref[...] = _mm(hb, wg_ref[...])


def _row_spec(tm, d):
    return pl.BlockSpec((None, tm, d), lambda b, i: (b, i, 0))


def _mod_spec(mod, tm, d):
    if mod.shape[1] == 1:
        return pl.BlockSpec((None, 1, d), lambda b, i: (b, 0, 0))
    return pl.BlockSpec((None, tm, d), lambda b, i: (b, i, 0))


def _full_spec(a):
    nd = a.ndim
    return pl.BlockSpec(a.shape, lambda b, i: (0,) * nd)


def _inproj_call(x, g, shift, scale, w_q, w_kvt, w_g, tm, layer, depth, kv_stacks):
    b, t, d = x.shape
    row = lambda width, dtype: jax.ShapeDtypeStruct((b, t, width), dtype)
    kvt = jax.ShapeDtypeStruct((depth, b, 2 * WIDTH, t), F32)
    kvt_spec = pl.BlockSpec((None, None, 2 * WIDTH, tm), lambda bb, i: (layer, bb, 0, i))
    operands = [x, g, shift, scale, w_q, w_kvt, w_g]
    in_specs = [_row_spec(tm, d), _full_spec(g), _mod_spec(shift, tm, d), _mod_spec(scale, tm, d),
                _full_spec(w_q), _full_spec(w_kvt), _full_spec(w_g)]
    aliases = {len(operands): 2, len(operands) + 1: 3}
    operands += list(kv_stacks)
    in_specs += [pl.BlockSpec(memory_space=pl.ANY)] * 2
    return pl.pallas_call(
        _inproj_kernel,
        out_shape=(row(WIDTH, MXU_DTYPE), row(WIDTH, MXU_DTYPE), kvt, kvt, row(w_g.shape[-1], F32)),
        grid=(b, t // tm),
        in_specs=in_specs,
        out_specs=(_row_spec(tm, WIDTH), _row_spec(tm, WIDTH), kvt_spec, kvt_spec, _row_spec(tm, w_g.shape[-1])),
        input_output_aliases=aliases,
        compiler_params=_params("arbitrary", "arbitrary"),
        name="inproj",
    )(*operands)


LANES = 128
HEAD_PAIRS = [slice((h // 2) * LANES, (h // 2 + 1) * LANES) for h in range(N_HEADS)]


def _own_lanes(h, rows):
    lane = lax.broadcasted_iota(jnp.int32, (rows, LANES), 1)
    return lane < HEAD_DIM if h % 2 == 0 else lane >= HEAD_DIM


def _spare_lane(h, rows, k):
    lane = lax.broadcasted_iota(jnp.int32, (rows, LANES), 1)
    return lane == (HEAD_DIM + k if h % 2 == 0 else k)


def _stage_kv(kvt_ref, ka_ref, vt_ref, tile, km_ref=None, key_extras=None):
    for c in range(ka_ref.shape[0]):
        k_rows = kvt_ref[:WIDTH, c * tile:(c + 1) * tile].T
        vt_ref[c] = kvt_ref[WIDTH:, c * tile:(c + 1) * tile].astype(MXU_DTYPE)
        if key_extras is None:
            ka_ref[c] = k_rows.astype(MXU_DTYPE)
            continue
        for h in range(N_HEADS):
            ka_ref[c, h] = jnp.where(_own_lanes(h, tile), k_rows[:, HEAD_PAIRS[h]],
                                     key_extras(h)).astype(MXU_DTYPE)
        if km_ref is not None:
            km_ref[c:c + 1, :] = (jnp.sum(k_rows, axis=0, keepdims=True) * (1.0 / tile)).astype(MXU_DTYPE)


def _sb_prompt_kernel(q_ref, kv_ref, o_ref, kb_ref, vt_ref, carry_ref, ot_ref, *, tile):
    i = pl.program_id(1)

    @pl.when(i == 0)
    def _():
        _stage_kv(kv_ref, kb_ref, vt_ref, tile)

    row = lax.broadcasted_iota(jnp.int32, (tile, tile), 0)
    col = lax.broadcasted_iota(jnp.int32, (tile, tile), 1)
    before = row < col
    later = jnp.where(before, 1.0, 0.0).astype(MXU_DTYPE)

    heads = [slice(h * HEAD_DIM, (h + 1) * HEAD_DIM) for h in range(N_HEADS)]

    def key_tile(j, diagonal):
        zs = [_mm_nt(kb_ref[j, :, hs], q_ref[:, hs]) for hs in heads]
        lfs, sums = [], []
        for z in zs:
            lf = _neg_softplus(z)
            lfs.append(lf)
            hi, lo = _split_hi_lo(jnp.where(before, lf, 0.0) if diagonal else lf)
            sums.append(_mm(later, hi) + _mm(later, lo))
        for h, (hs, z, lf, sm) in enumerate(zip(heads, zs, lfs, sums)):
            total = jnp.sum(jnp.where(before, lf, 0.0) if diagonal else lf, axis=0, keepdims=True)
            if diagonal:
                a = jnp.where(before, jnp.exp(z + lf + sm), 0.0)
                ot_ref[hs, :] = _mm(vt_ref[j, hs, :], a.astype(MXU_DTYPE))
                carry_ref[h:h + 1, :] = total
            else:
                carry = carry_ref[h:h + 1, :]
                a = jnp.exp(z + lf + (sm + carry))
                ot_ref[hs, :] += _mm(vt_ref[j, hs, :], a.astype(MXU_DTYPE))
                carry_ref[h:h + 1, :] = carry + total

    def alive():
        return (jnp.max(carry_ref[...]) > SB_DEAD).astype(jnp.int32)

    def walk(state):
        j, _ = state
        key_tile(j, False)
        return j - 1, alive()

    key_tile(i, True)
    lax.while_loop(lambda state: jnp.logical_and(state[0] >= 0, state[1] > 0), walk, (i - 1, alive()))
    o_ref[...] = ot_ref[...].T.astype(o_ref.dtype)


def _moba_prompt_kernel(q_ref, kv_ref, o_ref, ka_ref, vt_ref, km_ref, nb_ref, qa_ref, m_ref, l_ref, ot_ref, *,
                        tile):
    i = pl.program_id(1)
    n_blk = ka_ref.shape[0]
    assert tile <= 256
    tile_row = lax.broadcasted_iota(jnp.int32, (tile, LANES), 0).astype(F32)

    def key_extras(h):
        return jnp.where(_spare_lane(h, tile, 0), tile_row, jnp.where(_spare_lane(h, tile, 1), 1.0, 0.0))

    @pl.when(i == 0)
    def _():
        _stage_kv(kv_ref, ka_ref, vt_ref, tile, km_ref, key_extras)

    for h in range(N_HEADS):
        slope = ALIBI_SLOPES[h]
        extras = jnp.where(_spare_lane(h, tile, 0), slope,
                           jnp.where(_spare_lane(h, tile, 1), -slope * tile_row, 0.0))
        qa_ref[h] = jnp.where(_own_lanes(h, tile), q_ref[:, HEAD_PAIRS[h]].astype(F32), extras).astype(MXU_DTYPE)

    row = lax.broadcasted_iota(jnp.int32, (tile, tile), 0)
    col = lax.broadcasted_iota(jnp.int32, (tile, tile), 1)
    causal = row <= col
    blk = lax.broadcasted_iota(jnp.int32, (n_blk, tile), 0)
    heads = [slice(h * HEAD_DIM, (h + 1) * HEAD_DIM) for h in range(N_HEADS)]
    lanes = [slice(h * tile, (h + 1) * tile) for h in range(N_HEADS)]
    slope_lanes = jnp.concatenate([jnp.full((1, tile), s, F32) for s in ALIBI_SLOPES], axis=1)

    def scores(j, fix=lambda s: s):
        return jnp.concatenate([fix(_mm_nt(ka_ref[j, h], qa_ref[h])) for h in range(N_HEADS)], axis=1)

    for h, ls in enumerate(lanes):
        km_h = jnp.where(_own_lanes(h, n_blk), km_ref[:, HEAD_PAIRS[h]], jnp.zeros((), MXU_DTYPE))
        gate = _mm_nt(km_h, q_ref[:, HEAD_PAIRS[h]])
        rank = jnp.zeros((n_blk, tile), F32)
        for n2 in range(n_blk):
            g2 = gate[n2:n2 + 1, :]
            beats = (g2 > gate) | ((g2 == gate) & (n2 < blk))
            rank = rank + jnp.where(beats, 1.0, 0.0) * jnp.where(n2 < i, 1.0, 0.0)
        chosen = (blk < i) & (rank < MOBA_TOPK)
        nb_ref[:, ls] = jnp.where(chosen, 0.0, NEG)

    s = scores(i, lambda s: jnp.where(causal, s, NEG))
    m = jnp.max(s, axis=0, keepdims=True)
    p = jnp.exp(s - m)
    m_ref[...] = m
    l_ref[...] = jnp.sum(p, axis=0, keepdims=True)
    pb = p.astype(MXU_DTYPE)
    for hs, ls in zip(heads, lanes):
        ot_ref[hs, :] = _mm(vt_ref[i, hs, :], pb[:, ls])

    def past_block(j, c):
        row_bias = nb_ref[pl.ds(j, 1), :] + slope_lanes * (tile * (j - i)).astype(F32)
        for g in range(0, N_HEADS, MOBA_HEAD_GROUP):
            group = range(g, g + MOBA_HEAD_GROUP)
            gl = slice(g * tile, (g + MOBA_HEAD_GROUP) * tile)
            s = jnp.concatenate([_mm_nt(ka_ref[j, h], qa_ref[h]) for h in group], axis=1) + row_bias[:, gl]
            m = m_ref[:, gl]
            m_new = jnp.maximum(m, jnp.max(s, axis=0, keepdims=True))
            corr = jnp.exp(m - m_new)
            p = jnp.exp(s - m_new)
            l_ref[:, gl] = l_ref[:, gl] * corr + jnp.sum(p, axis=0, keepdims=True)
            m_ref[:, gl] = m_new
            pb = p.astype(MXU_DTYPE)
            for k, h in enumerate(group):
                ks = slice(k * tile, (k + 1) * tile)
                ot_ref[heads[h], :] = ot_ref[heads[h], :] * corr[:, ks] + _mm(vt_ref[j, heads[h], :], pb[:, ks])
        return c

    lax.fori_loop(0, i, past_block, 0)
    l = l_ref[...]
    for hs, ls in zip(heads, lanes):
        ot_ref[hs, :] = ot_ref[hs, :] / l[:, ls]
    o_ref[...] = ot_ref[...].T.astype(o_ref.dtype)


def _prompt_attention_call(kernel_fn, q, kv_stack, layer, tile, extra_scratch, name):
    b, t, _ = q.shape
    assert t % tile == 0
    n_t = t // tile
    scratch = extra_scratch(n_t, tile)
    scratch = scratch[:1] + [pltpu.VMEM((n_t, WIDTH, tile), MXU_DTYPE)] + scratch[1:] + [pltpu.VMEM((WIDTH, tile), F32)]
    return pl.pallas_call(
        functools.partial(kernel_fn, tile=tile),
        out_shape=jax.ShapeDtypeStruct((b, t, WIDTH), MXU_DTYPE),
        grid=(b, n_t),
        in_specs=[pl.BlockSpec((None, tile, WIDTH), lambda bb, i: (bb, i, 0)),
                  pl.BlockSpec((None, None, 2 * WIDTH, t), lambda bb, i: (layer, bb, 0, 0))],
        out_specs=pl.BlockSpec((None, tile, WIDTH), lambda bb, i: (bb, i, 0)),
        scratch_shapes=scratch,
        compiler_params=_params("arbitrary", "arbitrary"),
        name=name,
    )(q, kv_stack)


def _sb_prompt_call(q, kv_stack, layer):
    extra = lambda n_t, tile: [pltpu.VMEM((n_t, tile, WIDTH), MXU_DTYPE), pltpu.VMEM((N_HEADS, tile), F32)]
    return _prompt_attention_call(_sb_prompt_kernel, q, kv_stack, layer, SB_TILE, extra, "sb_prompt")


def _moba_prompt_call(q, kv_stack, layer):
    extra = lambda n_t, tile: [pltpu.VMEM((n_t, N_HEADS, tile, LANES), MXU_DTYPE),
                               pltpu.VMEM((n_t, WIDTH), MXU_DTYPE), pltpu.VMEM((n_t, N_HEADS * tile), F32),
                               pltpu.VMEM((N_HEADS, tile, LANES), MXU_DTYPE),
                               pltpu.VMEM((1, N_HEADS * tile), F32), pltpu.VMEM((1, N_HEADS * tile), F32)]
    return _prompt_attention_call(_moba_prompt_kernel, q, kv_stack, layer, MOBA_BLOCK, extra, "moba_prompt")


def _decode_kernel(pt_ref, layer_ref, qs_ref, qm_ref, kvtn_ref, sb_hbm_ref, mb_hbm_ref, osb_ref, omb_ref,
                   km_ref, kvn_ref, acc_ref, carry_ref, page_ref, page_sem, pages_ref, pages_sem, *,
                   n_seq, n_pages, n_tail, page):
    b = pl.program_id(0)
    slot = b % 2

    def page_copies(seq, into):
        first = seq * n_pages
        sources = [sb_hbm_ref.at[layer_ref[0], pt_ref[first + n_pages - n_tail + k]] for k in range(n_tail)]
        sources += [mb_hbm_ref.at[layer_ref[0], pt_ref[first + p]] for p in range(n_pages)]
        return [pltpu.make_async_copy(src, pages_ref.at[into, k], pages_sem.at[into])
                for k, src in enumerate(sources)]

    @pl.when(b == 0)
    def _():
        for copy in page_copies(0, 0):
            copy.start()

    @pl.when(b + 1 < n_seq)
    def _():
        for copy in page_copies(b + 1, 1 - slot):
            copy.start()

    for copy in page_copies(b, slot):
        copy.wait()
    sb_tail = [pages_ref.at[slot, k] for k in range(n_tail)]
    mb_pages = [pages_ref.at[slot, n_tail + p] for p in range(n_pages)]
    past = n_pages * page
    pages_per_blk = MOBA_BLOCK // page
    n_blk = past // MOBA_BLOCK

    @pl.when(b == 0)
    def _():
        kvn_ref[...] = kvtn_ref[...].T

    head_of_lane = lax.broadcasted_iota(jnp.int32, (N_HEADS, WIDTH), 1) // HEAD_DIM
    head_row = lax.broadcasted_iota(jnp.int32, (N_HEADS, WIDTH), 0)
    own_cols = head_of_lane == head_row

    def spread(q_row):
        return jnp.where(own_cols, q_row.astype(F32), 0.0)

    def gather_heads(x):
        return jnp.sum(jnp.where(own_cols, x, 0.0), axis=0, keepdims=True)

    qb = spread(qs_ref[...]).astype(MXU_DTYPE)
    r = lax.broadcasted_iota(jnp.int32, (page, 2 * page), 0)
    c = lax.broadcasted_iota(jnp.int32, (page, 2 * page), 1)
    later_and_total = jnp.where((r > c) | (c >= page), 1.0, 0.0).astype(MXU_DTYPE)

    def sb_pages_terms(page_refs, carry):
        zs = [_mm(qb, ref[:WIDTH, :].astype(MXU_DTYPE)) for ref in page_refs]
        lfs = [_neg_softplus(z) for z in zs]
        sums = []
        for lf in lfs:
            hi, lo = _split_hi_lo(lf)
            sums.append(_mm(hi, later_and_total) + _mm(lo, later_and_total))
        acc = jnp.zeros((N_HEADS, WIDTH), F32)
        for ref, z, lf, sm in zip(page_refs, zs, lfs, sums):
            a = jnp.exp(z + lf + (carry + sm[:, :page]))
            acc = acc + _mm_nt(a.astype(MXU_DTYPE), ref[WIDTH:, :].astype(MXU_DTYPE))
            carry = carry + sm[:, page:]
        return acc, carry

    acc, carry = sb_pages_terms(sb_tail[::-1], jnp.zeros((N_HEADS, page), F32))
    acc_ref[...] = acc
    carry_ref[...] = carry

    qb_m = spread(qm_ref[...]).astype(MXU_DTYPE)
    for n in range(n_blk):
        tot = mb_pages[n * pages_per_blk][:WIDTH, :]
        for p in range(n * pages_per_blk + 1, (n + 1) * pages_per_blk):
            tot = tot + mb_pages[p][:WIDTH, :]
        km_ref[:, n:n + 1] = jnp.sum(tot, axis=1, keepdims=True) * (1.0 / MOBA_BLOCK)
    gate = _mm(qb_m, km_ref[...].astype(MXU_DTYPE))
    blk = lax.broadcasted_iota(jnp.int32, (N_HEADS, n_blk), 1)
    rank = jnp.zeros((N_HEADS, n_blk), F32)
    for n2 in range(n_blk):
        g2 = gate[:, n2:n2 + 1]
        beats = (g2 > gate) | ((g2 == gate) & (n2 < blk))
        rank = rank + jnp.where(beats, 1.0, 0.0)
    not_chosen = jnp.where(rank < MOBA_TOPK, 0.0, NEG)

    slope = jnp.zeros((N_HEADS, page), F32)
    hrow = lax.broadcasted_iota(jnp.int32, (N_HEADS, page), 0)
    for h in range(N_HEADS):
        slope = jnp.where(hrow == h, ALIBI_SLOPES[h], slope)
    lane = lax.broadcasted_iota(jnp.int32, (N_HEADS, page), 1)

    kv_new = kvn_ref[pl.ds(b, 1), :].astype(MXU_DTYPE).astype(F32)
    s_own = jnp.sum(qb_m.astype(F32) * kv_new[:, :WIDTH], axis=1, keepdims=True)
    scores = []
    m = s_own
    for p in range(n_pages):
        dist = (past - p * page - lane).astype(F32)
        n = p // pages_per_blk
        s = _mm(qb_m, mb_pages[p][:WIDTH, :].astype(MXU_DTYPE)) - slope * dist + not_chosen[:, n:n + 1]
        scores.append(s)
        m = jnp.maximum(m, jnp.max(s, axis=1, keepdims=True))
    p_own = jnp.exp(s_own - m)
    l = p_own
    acc_m = p_own.astype(MXU_DTYPE).astype(F32) * kv_new[:, WIDTH:]
    for p in range(n_pages):
        w = jnp.exp(scores[p] - m)
        l = l + jnp.sum(w, axis=1, keepdims=True)
        acc_m = acc_m + _mm_nt(w.astype(MXU_DTYPE), mb_pages[p][WIDTH:, :].astype(MXU_DTYPE))
    omb_ref[...] = gather_heads(acc_m / l).astype(omb_ref.dtype)

    def alive():
        return (jnp.max(carry_ref[...]) > SB_DEAD).astype(jnp.int32)

    def older_page(state):
        p, _ = state
        fetch = pltpu.make_async_copy(sb_hbm_ref.at[layer_ref[0], pt_ref[b * n_pages + p]], page_ref, page_sem)
        fetch.start()
        fetch.wait()
        acc, carry = sb_pages_terms([page_ref], carry_ref[...])
        acc_ref[...] += acc
        carry_ref[...] = carry
        return p - 1, alive()

    lax.while_loop(lambda state: jnp.logical_and(state[0] >= 0, state[1] > 0), older_page,
                   (n_pages - n_tail - 1, alive()))
    osb_ref[...] = gather_heads(acc_ref[...]).astype(osb_ref.dtype)


def _decode_call(layer, page_table, q_sb, q_mb, kvt_mb_new, cache_sb, cache_mb):
    n_seq, n_pages = page_table.shape
    page = cache_sb.shape[-1]
    assert (n_pages * page) % MOBA_BLOCK == 0 and MOBA_BLOCK % page == 0
    n_tail = min(n_pages, SB_TAIL_KEYS // page)

    def seq_spec(width):
        return pl.BlockSpec((None, 1, width), lambda b, pt, lay: (b, 0, 0))

    grid_spec = pltpu.PrefetchScalarGridSpec(
        num_scalar_prefetch=2,
        grid=(n_seq,),
        in_specs=[seq_spec(WIDTH), seq_spec(WIDTH), pl.BlockSpec((2 * WIDTH, n_seq), lambda b, pt, lay: (0, 0)),
                  pl.BlockSpec(memory_space=pl.ANY), pl.BlockSpec(memory_space=pl.ANY)],
        out_specs=(seq_spec(WIDTH), seq_spec(WIDTH)),
        scratch_shapes=[pltpu.VMEM((WIDTH, n_pages * page // MOBA_BLOCK), F32),
                        pltpu.VMEM((n_seq, 2 * WIDTH), F32),
                        pltpu.VMEM((N_HEADS, WIDTH), F32),
                        pltpu.VMEM((N_HEADS, page), F32),
                        pltpu.VMEM((2 * WIDTH, page), F32),
                        pltpu.SemaphoreType.DMA(()),
                        pltpu.VMEM((2, n_tail + n_pages, 2 * WIDTH, page), F32),
                        pltpu.SemaphoreType.DMA((2,))],
    )
    out_shape = (jax.ShapeDtypeStruct((n_seq, 1, WIDTH), MXU_DTYPE),) * 2
    return pl.pallas_call(
        functools.partial(_decode_kernel, n_seq=n_seq, n_pages=n_pages, n_tail=n_tail, page=page),
        out_shape=out_shape,
        grid_spec=grid_spec,
        compiler_params=_params("arbitrary"),
        name="decode_attention",
    )(page_table.reshape(-1), jnp.full((1,), layer, jnp.int32), q_sb, q_mb, kvt_mb_new, cache_sb, cache_mb)


def _merge_kernel(osb_ref, omb_ref, gate_ref, x_ref, ga_ref, wps_ref, wpm_ref, wo_ref, xo_ref):
    d = x_ref.shape[-1]
    merged = (jax.nn.sigmoid(gate_ref[:, :d]) * _mm(osb_ref[...], wps_ref[...])
              + jax.nn.sigmoid(gate_ref[:, d:]) * _mm(omb_ref[...], wpm_ref[...]))
    xo_ref[...] = x_ref[...] + ga_ref[...] * _mm(merged.astype(MXU_DTYPE), wo_ref[...])


def _merge_call(o_sb, o_mb, gates, x, ga1, w_ps, w_pm, w_o, tm):
    b, t, d = x.shape
    return pl.pallas_call(
        _merge_kernel,
        out_shape=jax.ShapeDtypeStruct(x.shape, F32),
        grid=(b, t // tm),
        in_specs=[_row_spec(tm, WIDTH), _row_spec(tm, WIDTH), _row_spec(tm, gates.shape[-1]), _row_spec(tm, d),
                  _mod_spec(ga1, tm, d), _full_spec(w_ps), _full_spec(w_pm), _full_spec(w_o)],
        out_specs=_row_spec(tm, d),
        compiler_params=_params("arbitrary", "arbitrary"),
        name="merge_outproj",
    )(o_sb, o_mb, gates, x, ga1, w_ps, w_pm, w_o)


def _ffn_kernel(x_ref, g_ref, sh_ref, sc_ref, ga_ref, wgu_ref, wdn_ref, gout_ref, xo_ref, acc_ref, *, final):
    x = x_ref[...]
    hb = _rms_modulate(x, g_ref[...], sh_ref[...], sc_ref[...]).astype(MXU_DTYPE)
    d_ff = wdn_ref.shape[0]
    for c in range(d_ff // FFN_CHUNK):
        cols = slice(c * FFN_CHUNK, (c + 1) * FFN_CHUNK)
        gate = _mm(hb, wgu_ref[:, cols])
        up = _mm(hb, wgu_ref[:, d_ff + c * FFN_CHUNK:d_ff + (c + 1) * FFN_CHUNK])
        act = (gate * jax.nn.sigmoid(gate) * up).astype(MXU_DTYPE)
        part = _mm(act, wdn_ref[cols, :])
        if c == 0:
            acc_ref[...] = part
        else:
            acc_ref[...] += part
    y = x + ga_ref[...] * acc_ref[...]
    if final:
        y = y * lax.rsqrt(jnp.mean(y * y, axis=-1, keepdims=True) + RMS_EPS) * gout_ref[...]
    xo_ref[...] = y


def _ffn_call(x, g, shift, scale, ga2, w_gu, w_dn, g_out, final, tm):
    b, t, d = x.shape
    return pl.pallas_call(
        functools.partial(_ffn_kernel, final=final),
        out_shape=jax.ShapeDtypeStruct(x.shape, F32),
        grid=(b, t // tm),
        in_specs=[_row_spec(tm, d), _full_spec(g), _mod_spec(shift, tm, d), _mod_spec(scale, tm, d),
                  _mod_spec(ga2, tm, d), _full_spec(w_gu), _full_spec(w_dn), _full_spec(g_out)],
        out_specs=_row_spec(tm, d),
        scratch_shapes=[pltpu.VMEM((tm, d), F32)],
        compiler_params=_params("arbitrary", "arbitrary"),
        name="swiglu",
    )(x, g, shift, scale, ga2, w_gu, w_dn, g_out)


FFN_CHUNK = 256


def _row_tile(t):
    for tm in (512, 256, 128, 64, 32, 16, 8):
        if t % tm == 0:
            return tm
    return t


def kernel(x_prompt, x_sample, c_prompt, c_sample, cache_sb_kv, cache_moba_kv, page_table, w_ada, b_ada, g_mix,
           w_in, w_proj_sb, w_proj_moba, w_out, g_ffn, w_gate_up, w_down, g_final):
    n_b, t, d = x_prompt.shape
    n_s = x_sample.shape[0]
    depth, n_pool, page = cache_sb_kv.shape[:3]
    d_ff = w_down.shape[1]
    assert x_sample.shape[1] == 1 and d_ff % FFN_CHUNK == 0
    w = WIDTH

    w_in_b = w_in.astype(MXU_DTYPE)
    w_q = jnp.concatenate([w_in_b[:, :, 0:w], w_in_b[:, :, 3 * w:4 * w]], axis=2)
    w_kvt = jnp.concatenate([w_in_b[:, :, w:3 * w], w_in_b[:, :, 4 * w:6 * w]], axis=2).transpose(0, 2, 1)
    w_g = w_in_b[:, :, 6 * w:]
    w_ps_b = w_proj_sb.astype(MXU_DTYPE)
    w_pm_b = w_proj_moba.astype(MXU_DTYPE)
    w_o_b = w_out.astype(MXU_DTYPE)
    w_gu_b = w_gate_up.astype(MXU_DTYPE)
    w_dn_b = w_down.astype(MXU_DTYPE)

    to_pages = lambda cache: cache.transpose(0, 1, 3, 4, 5, 2).reshape(depth, n_pool, 2 * w, page)
    cache_sb_t, cache_mb_t = to_pages(cache_sb_kv), to_pages(cache_moba_kv)

    mod = _ada_call(jnp.concatenate([c_prompt, c_sample], axis=0), w_ada, b_ada)
    g_fin = g_final.reshape(1, d)

    xp = x_prompt
    xs = x_sample.reshape(1, n_s, d)
    tm_p, tm_s = _row_tile(t), _row_tile(n_s)
    kv_p = [jnp.zeros((depth, n_b, 2 * w, t), F32)] * 2
    kv_s = [jnp.zeros((depth, 1, 2 * w, n_s), F32)] * 2
    for l in range(depth):
        mod_p = [mod[l, :n_b, k * d:(k + 1) * d].reshape(n_b, 1, d) for k in range(6)]
        mod_s = [mod[l, n_b:, k * d:(k + 1) * d].reshape(1, n_s, d) for k in range(6)]
        g_mix_l, g_ffn_l = g_mix[l].reshape(1, d), g_ffn[l].reshape(1, d)

        qs, qm, *kv_p, gates = _inproj_call(xp, g_mix_l, mod_p[0], mod_p[1], w_q[l], w_kvt[l], w_g[l], tm_p,
                                            l, depth, kv_p)
        o_sb = _sb_prompt_call(qs, kv_p[0], l)
        o_mb = _moba_prompt_call(qm, kv_p[1], l)
        xp = _merge_call(o_sb, o_mb, gates, xp, mod_p[2], w_ps_b[l], w_pm_b[l], w_o_b[l], tm_p)
        xp = _ffn_call(xp, g_ffn_l, mod_p[3], mod_p[4], mod_p[5], w_gu_b[l], w_dn_b[l], g_fin, l == depth - 1, tm_p)

        qs, qm, *kv_s, gates = _inproj_call(xs, g_mix_l, mod_s[0], mod_s[1], w_q[l], w_kvt[l], w_g[l], tm_s,
                                            l, depth, kv_s)
        o_sb, o_mb = _decode_call(l, page_table, qs.reshape(n_s, 1, w), qm.reshape(n_s, 1, w), kv_s[1][l, 0],
                                  cache_sb_t, cache_mb_t)
        xs = _merge_call(o_sb.reshape(1, n_s, w), o_mb.reshape(1, n_s, w), gates, xs, mod_s[2],
                         w_ps_b[l], w_pm_b[l], w_o_b[l], tm_s)
        xs = _ffn_call(xs, g_ffn_l, mod_s[3], mod_s[4], mod_s[5], w_gu_b[l], w_dn_b[l], g_fin, l == depth - 1, tm_s)

    y_prompt, y_sample = xp, xs.reshape(n_s, 1, d)

    def kv_prompt(kvt):
        return kvt.reshape(depth, n_b, 2, N_HEADS, HEAD_DIM, t).transpose(0, 1, 5, 2, 3, 4)

    def kv_sample(kvt):
        return kvt.reshape(depth, 1, 2, N_HEADS, HEAD_DIM, n_s).transpose(0, 5, 1, 2, 3, 4)

    return (y_prompt, y_sample, kv_prompt(kv_p[0]), kv_prompt(kv_p[1]), kv_sample(kv_s[0]), kv_sample(kv_s[1]))
```
